```python
import jax, jax.numpy as jnp
from jax import lax
import numpy as np


D_MODEL = 1024
BATCH = 8
SEQ = 2048
DEPTH = 4
DEC_BATCH = 128
DEC_SEQ = 4
PAST_LEN = 16384
PAGE_SIZE = 128

D_A = D_MODEL // 4
D_B = 3 * D_MODEL // 8
D_C = D_MODEL - D_A - D_B
POOL_WINDOWS = (2, 4, 8, 16)
POOL_MAX = 16
POOL_GC = D_A // len(POOL_WINDOWS)
SGU_HEADS = 4
SGU_HD = D_B // SGU_HEADS
SGU_CHUNK = 128
CONV_K = 31
FFN_K = 3
D_FF = ((8 * D_MODEL // 3 + 127) // 128) * 128
D_IN = D_A + 2 * D_B + 2 * D_C
EPS = 1e-6

kernel_name = "hybrid_pool_sgu_conformer_decoder_step"


def rmsnorm(x, g):
    xf = x.astype(jnp.float32)
    r = lax.rsqrt(jnp.mean(xf * xf, axis=-1, keepdims=True) + EPS)
    return (xf * r).astype(x.dtype) * g


def layernorm(x, g, b):
    xf = x.astype(jnp.float32)
    mu = jnp.mean(xf, axis=-1, keepdims=True)
    var = jnp.mean(jnp.square(xf - mu), axis=-1, keepdims=True)
    return ((xf - mu) * lax.rsqrt(var + EPS)).astype(x.dtype) * g + b


def pool_mix(xa, prev, pos0, w_pool, scale):
    B, T, _ = xa.shape
    P = POOL_MAX - 1
    xp = jnp.concatenate([prev, xa], axis=1).astype(jnp.float32)
    cs = jnp.concatenate([jnp.zeros((B, 1, D_A), jnp.float32), jnp.cumsum(xp, axis=1)], axis=1)
    pos = pos0 + jnp.arange(T)
    outs = []
    for g, w in enumerate(POOL_WINDOWS):
        sl = slice(g * POOL_GC, (g + 1) * POOL_GC)
        s = cs[:, P + 1:P + 1 + T, sl] - cs[:, P + 1 - w:P + 1 - w + T, sl]
        cnt = jnp.minimum(pos + 1, w).astype(jnp.float32)
        outs.append(s / cnt[None, :, None])
    pooled = jnp.concatenate(outs, axis=-1) - xp[:, P:]
    pooled = pooled.reshape(B, T, len(POOL_WINDOWS), POOL_GC)
    mixed = jnp.einsum('btgc,gcd->btgd', pooled, w_pool.astype(jnp.float32)).reshape(B, T, D_A)
    out = (mixed * scale.astype(jnp.float32)).astype(xa.dtype)
    return out, xp[:, -P:].astype(xa.dtype)


def spatial_gate(u, v, w_s, b_s):
    B, T, _ = v.shape
    L = min(T, SGU_CHUNK)
    n = T // L
    mask = jnp.tril(jnp.ones((L, L), dtype=bool))
    w = jnp.where(mask[None], w_s[:, :L, :L], 0)
    vh = v.reshape(B, n, L, SGU_HEADS, SGU_HD)
    s = jnp.einsum('hts,bnshc->bnthc', w, vh) + b_s[:, :L].T[None, None, :, :, None]
    return u * s.reshape(B, T, D_B)


def causal_dwconv(x, prev, w, b):
    K, C = w.shape
    xp = jnp.concatenate([prev, x], axis=1)
    y = lax.conv_general_dilated(xp, w[:, None, :], window_strides=(1,), padding='VALID',
                                 dimension_numbers=('NWC', 'WIO', 'NWC'),
                                 feature_group_count=C) + b
    return y, xp[:, -(K - 1):]


def trunk(x, pool_prev, conv_prev, ffn_prev, pos0, p):
    pool_new, conv_new, ffn_new, v_rows = [], [], [], []
    for l in range(DEPTH):
        h = rmsnorm(x, p['norm1'][l])
        z = h @ p['w_in'][l]
        za = z[..., :D_A]
        zu = z[..., D_A:D_A + D_B]
        zv = z[..., D_A + D_B:D_A + 2 * D_B]
        zc = z[..., D_A + 2 * D_B:]
        a_out, pst = pool_mix(za, pool_prev[l], pos0, p['pool_w'][l], p['pool_scale'][l])
        b_out = spatial_gate(zu, zv, p['sgu_w'][l], p['sgu_b'][l])
        c_in = zc[..., :D_C] * jax.nn.sigmoid(zc[..., D_C:])
        c_conv, cst = causal_dwconv(c_in, conv_prev[l], p['conv_w'][l], p['conv_b'][l])
        c_out = jax.nn.silu(layernorm(c_conv, p['cnorm_g'][l], p['cnorm_b'][l]))
        x = x + jnp.concatenate([a_out, b_out, c_out], axis=-1) @ p['w_out'][l]
        h2 = rmsnorm(x, p['norm2'][l])
        up = h2 @ p['w_up'][l]
        gate, val = up[..., :D_FF], up[..., D_FF:]
        gate_c, fst = causal_dwconv(gate, ffn_prev[l], p['ffn_conv_w'][l], p['ffn_conv_b'][l])
        x = x + (jax.nn.silu(gate_c) * val) @ p['w_down'][l]
        pool_new.append(pst)
        conv_new.append(cst)
        ffn_new.append(fst)
        v_rows.append(zv)
    y = rmsnorm(x, p['final_norm'])
    return y, jnp.stack(pool_new), jnp.stack(conv_new), jnp.stack(ffn_new), jnp.stack(v_rows)


def setup_inputs(seed: int = 0) -> dict:
    key = jax.random.key(seed)
    ks = jax.random.split(key, 24)
    f32 = jnp.float32
    nrm = lambda k, s, sc: (jax.random.normal(k, s, f32) * sc)
    return {
        'x_prompt': nrm(ks[0], (BATCH, SEQ, D_MODEL), 1.0),
        'x_sample': nrm(ks[1], (DEC_BATCH, DEC_SEQ, D_MODEL), 1.0),
        'state_pool': nrm(ks[2], (DEPTH, DEC_BATCH, POOL_MAX - 1, D_A), 1.0),
        'state_conv': nrm(ks[3], (DEPTH, DEC_BATCH, CONV_K - 1, D_C), 0.5),
        'state_ffn_conv': nrm(ks[4], (DEPTH, DEC_BATCH, FFN_K - 1, D_FF), 1.0),
        'norm1': 1.0 + nrm(ks[5], (DEPTH, D_MODEL), 0.02),
        'w_in': nrm(ks[6], (DEPTH, D_MODEL, D_IN), D_MODEL ** -0.5),
        'pool_w': nrm(ks[7], (DEPTH, len(POOL_WINDOWS), POOL_GC, POOL_GC), POOL_GC ** -0.5),
        'pool_scale': 1.0 + nrm(ks[8], (DEPTH, D_A), 0.1),
        'sgu_w': nrm(ks[9], (DEPTH, SGU_HEADS, SGU_CHUNK, SGU_CHUNK), SGU_CHUNK ** -0.5),
        'sgu_b': 1.0 + nrm(ks[10], (DEPTH, SGU_HEADS, SGU_CHUNK), 0.1),
        'conv_w': nrm(ks[11], (DEPTH, CONV_K, D_C), CONV_K ** -0.5),
        'conv_b': nrm(ks[12], (DEPTH, D_C), 0.02),
        'cnorm_g': 1.0 + nrm(ks[13], (DEPTH, D_C), 0.02),
        'cnorm_b': nrm(ks[14], (DEPTH, D_C), 0.02),
        'w_out': nrm(ks[15], (DEPTH, D_MODEL, D_MODEL), D_MODEL ** -0.5),
        'norm2': 1.0 + nrm(ks[16], (DEPTH, D_MODEL), 0.02),
        'w_up': nrm(ks[17], (DEPTH, D_MODEL, 2 * D_FF), D_MODEL ** -0.5),
        'ffn_conv_w': nrm(ks[18], (DEPTH, FFN_K, D_FF), FFN_K ** -0.5),
        'ffn_conv_b': nrm(ks[19], (DEPTH, D_FF), 0.02),
        'w_down': nrm(ks[20], (DEPTH, D_FF, D_MODEL), D_FF ** -0.5),
        'final_norm': 1.0 + nrm(ks[21], (D_MODEL,), 0.02),
    }


def reference(x_prompt, x_sample, state_pool, state_conv, state_ffn_conv,
              norm1, w_in, pool_w, pool_scale, sgu_w, sgu_b, conv_w, conv_b,
              cnorm_g, cnorm_b, w_out, norm2, w_up, ffn_conv_w, ffn_conv_b,
              w_down, final_norm):
    p = dict(norm1=norm1, w_in=w_in, pool_w=pool_w, pool_scale=pool_scale,
             sgu_w=sgu_w, sgu_b=sgu_b, conv_w=conv_w, conv_b=conv_b,
             cnorm_g=cnorm_g, cnorm_b=cnorm_b, w_out=w_out, norm2=norm2,
             w_up=w_up, ffn_conv_w=ffn_conv_w, ffn_conv_b=ffn_conv_b,
             w_down=w_down, final_norm=final_norm)
    B = x_prompt.shape[0]
    dt = x_prompt.dtype
    pool0 = jnp.zeros((DEPTH, B, POOL_MAX - 1, D_A), dt)
    conv0 = jnp.zeros((DEPTH, B, CONV_K - 1, D_C), dt)
    ffn0 = jnp.zeros((DEPTH, B, FFN_K - 1, D_FF), dt)
    y_prompt, pool_p, conv_p, ffn_p, _ = trunk(x_prompt, pool0, conv0, ffn0, 0, p)
    y_sample, pool_s, conv_s, ffn_s, v_s = trunk(x_sample, state_pool, state_conv,
                                                  state_ffn_conv, PAST_LEN, p)
    return (y_prompt, y_sample, pool_p, pool_s, conv_p, conv_s, ffn_p, ffn_s, v_s)
```

```python
import functools

import jax
import jax.numpy as jnp
from jax import lax
from jax.experimental import pallas as pl
from jax.experimental.pallas import tpu as pltpu

POOL_WINDOWS = (2, 4, 8, 16)
SGU_HEADS = 4
SGU_CHUNK = 128
PAST_LEN = 16384
EPS = 1e-6

LANES = 128
SUBLANES = 8
MXU_DIM = 256
VMEM_LIMIT_BYTES = 60 * 1024 * 1024

PROMPT_ROWS = 512
SAMPLE_SEQS = 64
ROW_BLOCK = 64
FFN_CHUNK = 4 * MXU_DIM

F32 = jnp.float32
BF16 = jnp.bfloat16


def _round_up(n, m):
    return -(-n // m) * m


def _rms_scale(x):
    return lax.rsqrt(jnp.mean(x * x, axis=-1, keepdims=True) + EPS)


def _silu(x):
    return x * jax.nn.sigmoid(x)


def _layernorm_silu(y, g, b):
    mu = jnp.mean(y, axis=-1, keepdims=True)
    d = y - mu
    var = jnp.mean(d * d, axis=-1, keepdims=True)
    return _silu(d * lax.rsqrt(var + EPS) * g + b)


def _dot(a, b):
    return jnp.dot(a, b, preferred_element_type=F32)


def _ffn_chunks(d_ff):
    return [(c0, min(FFN_CHUNK, d_ff - c0)) for c0 in range(0, d_ff, FFN_CHUNK)]


def _pool_window_select(snapshots, lane_group):
    out = snapshots[POOL_WINDOWS[0]]
    for g, w in enumerate(POOL_WINDOWS[1:], start=1):
        out = jnp.where(lane_group >= g, snapshots[w], out)
    return out


def _lane_window(lane_group):
    w = jnp.full(lane_group.shape, POOL_WINDOWS[0], jnp.int32)
    for g, win in enumerate(POOL_WINDOWS[1:], start=1):
        w = jnp.where(lane_group >= g, win, w)
    return w


def _prompt_layer_body(
        x_ref, g1_ref, w_in_ref, pool_w_ref, pool_scale_ref, sgu_w_ref, sgu_b_ref,
        conv_w_ref, conv_b_ref, cn_g_ref, cn_b_ref, w_out_ref, g2_ref,
        w_gate_ref, w_val_ref, fconv_w_ref, fconv_b_ref, w_down_ref, fnorm_ref,
        xo_ref, pool_tail_ref, conv_tail_ref, ffn_tail_ref,
        z_buf, za_buf, c_buf, g_halo, g_buf, val_buf, act_buf, mix_buf, acc_buf,
        *, tm, d_a, d_b, d_c, d_ff, pool_halo, conv_halo, ffn_halo, conv_k, ffn_k,
        apply_final_norm):
    s = pl.program_id(1)
    pool_gc = d_a // len(POOL_WINDOWS)
    pool_max = max(POOL_WINDOWS)

    @pl.when(s == 0)
    def _zero_left_state():
        za_buf[0:pool_halo, :] = jnp.zeros((pool_halo, d_a), F32)
        c_buf[0:conv_halo, :] = jnp.zeros((conv_halo, d_c), F32)
        g_halo[...] = jnp.zeros(g_halo.shape, F32)

    x = x_ref[...]
    h = (x * _rms_scale(x) * g1_ref[...]).astype(BF16)
    z_buf[...] = _dot(h, w_in_ref[...])
    o_u, o_v, o_c, o_g = d_a, d_a + d_b, d_a + 2 * d_b, d_a + 2 * d_b + d_c

    za_buf[pool_halo:pool_halo + tm, :] = z_buf[:, 0:d_a]
    lane_group = lax.broadcasted_iota(jnp.int32, (ROW_BLOCK, d_a), 1) // pool_gc
    lane_w = _lane_window(lane_group)
    row = lax.broadcasted_iota(jnp.int32, (ROW_BLOCK, d_a), 0)
    for r0 in range(0, tm, ROW_BLOCK):
        cur = za_buf[pool_halo + r0:pool_halo + r0 + ROW_BLOCK, :]
        acc = cur
        snapshots = {}
        for i in range(1, pool_max):
            lo = pool_halo + r0 - i
            acc = acc + za_buf[lo:lo + ROW_BLOCK, :]
            if i + 1 in POOL_WINDOWS:
                snapshots[i + 1] = acc
        cnt = jnp.minimum(s * tm + r0 + row + 1, lane_w).astype(F32)
        pooled = _pool_window_select(snapshots, lane_group) / cnt - cur
        mix_buf[r0:r0 + ROW_BLOCK, 0:d_a] = pooled.astype(BF16)
    mixed = _dot(mix_buf[:, 0:d_a], pool_w_ref[...]) * pool_scale_ref[...]
    mix_buf[:, 0:d_a] = mixed.astype(BF16)
    pool_tail_ref[...] = za_buf[tm:tm + pool_halo, :]
    za_buf[0:pool_halo, :] = za_buf[tm:tm + pool_halo, :]

    n_hd = SGU_HEADS * SGU_CHUNK
    w_row = lax.broadcasted_iota(jnp.int32, (n_hd, SGU_CHUNK), 0) % SGU_CHUNK
    w_col = lax.broadcasted_iota(jnp.int32, (n_hd, SGU_CHUNK), 1)
    w_tril = jnp.where(w_col <= w_row, sgu_w_ref[...], 0.0).astype(BF16)
    head_of_lane = lax.broadcasted_iota(jnp.int32, (SGU_CHUNK, d_b), 1) // (d_b // SGU_HEADS)
    for r0 in range(0, tm, SGU_CHUNK):
        v = z_buf[r0:r0 + SGU_CHUNK, o_v:o_c].astype(BF16)
        per_head = _dot(w_tril, v)
        gate = per_head[0:SGU_CHUNK]
        for hd in range(1, SGU_HEADS):
            gate = jnp.where(head_of_lane >= hd,
                             per_head[hd * SGU_CHUNK:(hd + 1) * SGU_CHUNK], gate)
        b_out = z_buf[r0:r0 + SGU_CHUNK, o_u:o_v] * (gate + sgu_b_ref[...])
        mix_buf[r0:r0 + SGU_CHUNK, d_a:d_a + d_b] = b_out.astype(BF16)

    c_buf[conv_halo:conv_halo + tm, :] = z_buf[:, o_c:o_g] * jax.nn.sigmoid(z_buf[:, o_g:o_g + d_c])
    for r0 in range(0, tm, ROW_BLOCK):
        acc = jnp.broadcast_to(conv_b_ref[...], (ROW_BLOCK, d_c))
        for k in range(conv_k):
            lo = conv_halo - (conv_k - 1) + k + r0
            acc = acc + conv_w_ref[k:k + 1, :] * c_buf[lo:lo + ROW_BLOCK, :]
        c_out = _layernorm_silu(acc, cn_g_ref[...], cn_b_ref[...])
        mix_buf[r0:r0 + ROW_BLOCK, d_a + d_b:d_a + d_b + d_c] = c_out.astype(BF16)
    conv_tail_ref[...] = c_buf[tm:tm + conv_halo, :]
    c_buf[0:conv_halo, :] = c_buf[tm:tm + conv_halo, :]

    x1 = x + _dot(mix_buf[...], w_out_ref[...])
    acc_buf[...] = x1

    h2 = (x1 * _rms_scale(x1) * g2_ref[...]).astype(BF16)
    for c0, cw in _ffn_chunks(d_ff):
        g_buf[0:ffn_halo, 0:cw] = g_halo[:, c0:c0 + cw]
        g_buf[ffn_halo:ffn_halo + tm, 0:cw] = _dot(h2, w_gate_ref[:, c0:c0 + cw])
        g_halo[:, c0:c0 + cw] = g_buf[tm:tm + ffn_halo, 0:cw]
        val_buf[:, 0:cw] = _dot(h2, w_val_ref[:, c0:c0 + cw])
        gate_c = jnp.broadcast_to(fconv_b_ref[:, c0:c0 + cw], (tm, cw))
        for k in range(ffn_k):
            lo = ffn_halo - (ffn_k - 1) + k
            gate_c = gate_c + fconv_w_ref[k:k + 1, c0:c0 + cw] * g_buf[lo:lo + tm, 0:cw]
        act_buf[:, 0:cw] = (_silu(gate_c) * val_buf[:, 0:cw]).astype(BF16)
        acc_buf[...] += _dot(act_buf[:, 0:cw], w_down_ref[c0:c0 + cw, :])
    ffn_tail_ref[...] = g_halo[...]

    x2 = acc_buf[...]
    if apply_final_norm:
        x2 = x2 * _rms_scale(x2) * fnorm_ref[...]
    xo_ref[...] = x2


_LAYER_WEIGHTS = ['norm1', 'w_in', 'pool_w', 'pool_scale', 'sgu_w', 'sgu_b', 'conv_w', 'conv_b',
                  'cnorm_g', 'cnorm_b', 'w_out', 'norm2', 'w_gate', 'w_val', 'ffn_conv_w',
                  'ffn_conv_b', 'w_down']


def _resident_spec(arr, layer, n_grid):
    zeros = (0,) * (arr.ndim - 1)
    return pl.BlockSpec((None,) + arr.shape[1:], lambda *_: (layer,) + zeros,
                        pipeline_mode=pl.Buffered(1))


def _prompt_layer(x, layer, wts, *, apply_final_norm):
    n_b, n_s, d = x.shape
    tm = PROMPT_ROWS
    d_a, d_c, d_ff = wts['pool_w'].shape[-1], wts['conv_w'].shape[-1], wts['w_gate'].shape[-1]
    d_b = wts['sgu_b'].shape[-1]
    conv_k, ffn_k = wts['conv_w'].shape[1], wts['ffn_conv_w'].shape[1]
    pool_halo = _round_up(max(POOL_WINDOWS) - 1, SUBLANES)
    conv_halo = _round_up(conv_k - 1, SUBLANES)
    ffn_halo = _round_up(ffn_k - 1, SUBLANES)
    assert n_s % tm == 0 and tm % SGU_CHUNK == 0 and tm % ROW_BLOCK == 0

    operands = [wts[n] for n in _LAYER_WEIGHTS]
    in_specs = ([pl.BlockSpec((None, tm, d), lambda b, s: (b, s, 0))]
                + [_resident_spec(a, layer, 2) for a in operands]
                + [pl.BlockSpec(wts['final_norm'].shape, lambda b, s: (0, 0),
                                pipeline_mode=pl.Buffered(1))])
    out_shape = [jax.ShapeDtypeStruct((n_b, n_s, d), F32),
                 jax.ShapeDtypeStruct((n_b, pool_halo, d_a), F32),
                 jax.ShapeDtypeStruct((n_b, conv_halo, d_c), F32),
                 jax.ShapeDtypeStruct((n_b, ffn_halo, d_ff), F32)]
    out_specs = [pl.BlockSpec((None, tm, d), lambda b, s: (b, s, 0)),
                 pl.BlockSpec((None, pool_halo, d_a), lambda b, s: (b, 0, 0)),
                 pl.BlockSpec((None, conv_halo, d_c), lambda b, s: (b, 0, 0)),
                 pl.BlockSpec((None, ffn_halo, d_ff), lambda b, s: (b, 0, 0))]
    cw_max = min(FFN_CHUNK, d_ff)
    scratch = [pltpu.VMEM((tm, wts['w_in'].shape[-1]), F32),
               pltpu.VMEM((pool_halo + tm, d_a), F32),
               pltpu.VMEM((conv_halo + tm, d_c), F32),
               pltpu.VMEM((ffn_halo, d_ff), F32),
               pltpu.VMEM((ffn_halo + tm, cw_max), F32),
               pltpu.VMEM((tm, cw_max), F32),
               pltpu.VMEM((tm, cw_max), BF16),
               pltpu.VMEM((tm, d), BF16),
               pltpu.VMEM((tm, d), F32)]
    body = functools.partial(
        _prompt_layer_body, tm=tm, d_a=d_a, d_b=d_b, d_c=d_c, d_ff=d_ff,
        pool_halo=pool_halo, conv_halo=conv_halo, ffn_halo=ffn_halo,
        conv_k=conv_k, ffn_k=ffn_k, apply_final_norm=apply_final_norm)
    return pl.pallas_call(
        body, grid=(n_b, n_s // tm), in_specs=in_specs, out_specs=out_specs,
        out_shape=out_shape, scratch_shapes=scratch, name=f'prompt_layer_{layer}',
        compiler_params=pltpu.CompilerParams(
            dimension_semantics=('arbitrary', 'arbitrary'),
            vmem_limit_bytes=VMEM_LIMIT_BYTES),
    )(x, *operands, wts['final_norm'])


def _sample_layer_body(
        x_ref, pool_st_ref, conv_st_ref, ffn_st_ref,
        g1_ref, w_in_ref, pool_w_ref, pool_scale_ref, sgu_w_ref, sgu_b_ref,
        conv_w_ref, conv_b_ref, cn_g_ref, cn_b_ref, w_out_ref, g2_ref,
        w_gate_ref, w_val_ref, fconv_w_ref, fconv_b_ref, w_down_ref, fnorm_ref,
        xo_ref, za_ref, cin_ref, gate_ref, v_ref,
        z_buf, g_buf, mix_buf, acc_buf,
        *, n_t, n_seq, d_a, d_b, d_c, d_ff, conv_k, ffn_k, apply_final_norm):
    pool_gc = d_a // len(POOL_WINDOWS)
    pool_max = max(POOL_WINDOWS)
    n_rows = n_t * n_seq

    def slab(t):
        return slice(t * n_seq, (t + 1) * n_seq)

    x = jnp.concatenate([x_ref[t] for t in range(n_t)], axis=0)
    h = (x * _rms_scale(x) * g1_ref[...]).astype(BF16)
    z_buf[...] = _dot(h, w_in_ref[...])
    o_u, o_v, o_c, o_g = d_a, d_a + d_b, d_a + 2 * d_b, d_a + 2 * d_b + d_c
    for t in range(n_t):
        za_ref[t] = z_buf[slab(t), 0:d_a]
        v_ref[t] = z_buf[slab(t), o_v:o_c]

    n_prev = pool_max - 1
    lane_group = lax.broadcasted_iota(jnp.int32, (n_seq, d_a), 1) // pool_gc
    lane_w = _lane_window(lane_group)

    def pool_row(j):
        return pool_st_ref[j] if j < n_prev else z_buf[slab(j - n_prev), 0:d_a]

    for t in range(n_t):
        cur = pool_row(n_prev + t)
        acc = cur
        snapshots = {}
        for i in range(1, pool_max):
            acc = acc + pool_row(n_prev + t - i)
            if i + 1 in POOL_WINDOWS:
                snapshots[i + 1] = acc
        cnt = jnp.minimum(PAST_LEN + t + 1, lane_w).astype(F32)
        pooled = _pool_window_select(snapshots, lane_group) / cnt - cur
        mix_buf[slab(t), 0:d_a] = pooled.astype(BF16)
    mixed = _dot(mix_buf[:, 0:d_a], pool_w_ref[...]) * pool_scale_ref[...]
    mix_buf[:, 0:d_a] = mixed.astype(BF16)

    for t in range(n_t):
        gate = jnp.broadcast_to(sgu_b_ref[t:t + 1, :], (n_seq, d_b))
        for sp in range(t + 1):
            gate = gate + sgu_w_ref[t * n_t + sp:t * n_t + sp + 1, :] * z_buf[slab(sp), o_v:o_c]
        mix_buf[slab(t), d_a:d_a + d_b] = (z_buf[slab(t), o_u:o_v] * gate).astype(BF16)

    for t in range(n_t):
        cin_ref[t] = z_buf[slab(t), o_c:o_g] * jax.nn.sigmoid(z_buf[slab(t), o_g:o_g + d_c])

    def conv_row(j):
        return conv_st_ref[j] if j < conv_k - 1 else cin_ref[j - (conv_k - 1)]

    for t in range(n_t):
        acc = jnp.broadcast_to(conv_b_ref[...], (n_seq, d_c))
        for k in range(conv_k):
            acc = acc + conv_w_ref[k:k + 1, :] * conv_row(t + k)
        c_out = _layernorm_silu(acc, cn_g_ref[...], cn_b_ref[...])
        mix_buf[slab(t), d_a + d_b:d_a + d_b + d_c] = c_out.astype(BF16)

    x1 = x + _dot(mix_buf[...], w_out_ref[...])
    acc_buf[...] = x1

    h2 = (x1 * _rms_scale(x1) * g2_ref[...]).astype(BF16)
    n_keep = ffn_k - 1
    for c0, cw in _ffn_chunks(d_ff):
        for k in range(n_keep):
            g_buf[slab(k), 0:cw] = ffn_st_ref[k, :, c0:c0 + cw]
        g_buf[n_keep * n_seq:n_keep * n_seq + n_rows, 0:cw] = _dot(h2, w_gate_ref[:, c0:c0 + cw])
        for k in range(n_keep):
            gate_ref[k, :, c0:c0 + cw] = g_buf[slab(n_t + k), 0:cw]
        val = _dot(h2, w_val_ref[:, c0:c0 + cw])
        gate_c = jnp.broadcast_to(fconv_b_ref[:, c0:c0 + cw], (n_rows, cw))
        for k in range(ffn_k):
            gate_c = gate_c + fconv_w_ref[k:k + 1, c0:c0 + cw] * g_buf[k * n_seq:k * n_seq + n_rows, 0:cw]
        act = (_silu(gate_c) * val).astype(BF16)
        acc_buf[...] += _dot(act, w_down_ref[c0:c0 + cw, :])

    x2 = acc_buf[...]
    if apply_final_norm:
        x2 = x2 * _rms_scale(x2) * fnorm_ref[...]
    for t in range(n_t):
        xo_ref[t] = x2[slab(t)]


def _sample_layer(x, pool_st, conv_st, ffn_st, layer, wts, *, apply_final_norm):
    n_t, n_all, d = x.shape
    n_seq = min(SAMPLE_SEQS, n_all)
    d_a, d_c, d_ff = wts['pool_w'].shape[-1], wts['conv_w'].shape[-1], wts['w_gate'].shape[-1]
    d_b = wts['sgu_b'].shape[-1]
    conv_k, ffn_k = wts['conv_w'].shape[1], wts['ffn_conv_w'].shape[1]
    n_keep = ffn_k - 1
    assert n_all % n_seq == 0 and n_seq % SUBLANES == 0
    assert n_keep <= n_t <= min(conv_k - 1, max(POOL_WINDOWS) - 1)

    def state_spec(arr):
        return pl.BlockSpec((None, arr.shape[1], n_seq, arr.shape[3]), lambda i: (layer, 0, i, 0))

    def rows_spec(n_time, ch):
        return pl.BlockSpec((n_time, n_seq, ch), lambda i: (0, i, 0))

    names = [n if n not in ('sgu_w', 'sgu_b') else n + '_new' for n in _LAYER_WEIGHTS]
    operands = [wts[n] for n in names]
    in_specs = ([rows_spec(n_t, d), state_spec(pool_st), state_spec(conv_st), state_spec(ffn_st)]
                + [_resident_spec(a, layer, 1) for a in operands]
                + [pl.BlockSpec(wts['final_norm'].shape, lambda i: (0, 0),
                                pipeline_mode=pl.Buffered(1))])
    out_dims = [(n_t, d), (n_t, d_a), (n_t, d_c), (n_keep, d_ff), (n_t, d_b)]
    out_shape = [jax.ShapeDtypeStruct((nt, n_all, ch), F32) for nt, ch in out_dims]
    out_specs = [rows_spec(nt, ch) for nt, ch in out_dims]
    n_rows = n_t * n_seq
    cw_max = min(FFN_CHUNK, d_ff)
    scratch = [pltpu.VMEM((n_rows, wts['w_in'].shape[-1]), F32),
               pltpu.VMEM((n_keep * n_seq + n_rows, cw_max), F32),
               pltpu.VMEM((n_rows, d), BF16),
               pltpu.VMEM((n_rows, d), F32)]
    body = functools.partial(
        _sample_layer_body, n_t=n_t, n_seq=n_seq, d_a=d_a, d_b=d_b, d_c=d_c, d_ff=d_ff,
        conv_k=conv_k, ffn_k=ffn_k, apply_final_norm=apply_final_norm)
    return pl.pallas_call(
        body, grid=(n_all // n_seq,), in_specs=in_specs, out_specs=out_specs,
        out_shape=out_shape, scratch_shapes=scratch, name=f'sample_layer_{layer}',
        compiler_params=pltpu.CompilerParams(
            dimension_semantics=('arbitrary',), vmem_limit_bytes=VMEM_LIMIT_BYTES),
    )(x, pool_st, conv_st, ffn_st, *operands, wts['final_norm'])


def _block_diag(w):
    depth, n_g, c, _ = w.shape
    eye = jnp.eye(n_g, dtype=w.dtype)
    return jnp.einsum('lgcd,gh->lgchd', w, eye).reshape(depth, n_g * c, n_g * c)


def kernel(x_prompt, x_sample, state_pool, state_conv, state_ffn_conv, norm1, w_in, pool_w, pool_scale, sgu_w, sgu_b, conv_w, conv_b, cnorm_g, cnorm_b, w_out, norm2, w_up, ffn_conv_w, ffn_conv_b, w_down, final_norm):
    depth = norm1.shape[0]
    d_ff = w_down.shape[1]
    d_b = (w_in.shape[-1] - pool_scale.shape[-1] - 2 * conv_w.shape[-1]) // 2
    hd = d_b // SGU_HEADS
    n_t = x_sample.shape[1]

    row = lambda a: a[:, None, :]
    wts = dict(
        norm1=row(norm1), w_in=w_in.astype(BF16),
        pool_w=_block_diag(pool_w).astype(BF16), pool_scale=row(pool_scale),
        sgu_w=sgu_w.reshape(depth, SGU_HEADS * SGU_CHUNK, SGU_CHUNK),
        sgu_b=jnp.repeat(jnp.swapaxes(sgu_b, 1, 2), hd, axis=-1),
        conv_w=conv_w, conv_b=row(conv_b), cnorm_g=row(cnorm_g), cnorm_b=row(cnorm_b),
        w_out=w_out.astype(BF16), norm2=row(norm2),
        w_gate=w_up[:, :, :d_ff].astype(BF16), w_val=w_up[:, :, d_ff:].astype(BF16),
        ffn_conv_w=ffn_conv_w, ffn_conv_b=row(ffn_conv_b), w_down=w_down.astype(BF16),
        final_norm=final_norm[None, :])
    w_new = jnp.transpose(sgu_w[:, :, :n_t, :n_t], (0, 2, 3, 1)).reshape(depth, n_t * n_t, SGU_HEADS)
    wts['sgu_w_new'] = jnp.repeat(w_new, hd, axis=-1)
    wts['sgu_b_new'] = jnp.repeat(jnp.swapaxes(sgu_b[:, :, :n_t], 1, 2), hd, axis=-1)

    x = x_prompt
    pool_p, conv_p, ffn_p = [], [], []
    for l in range(depth):
        x, pt, ct, ft = _prompt_layer(x, l, wts, apply_final_norm=(l == depth - 1))
        pool_p.append(pt[:, pt.shape[1] - (max(POOL_WINDOWS) - 1):])
        conv_p.append(ct[:, ct.shape[1] - (conv_w.shape[1] - 1):])
        ffn_p.append(ft[:, ft.shape[1] - (ffn_conv_w.shape[1] - 1):])
    y_prompt = x

    time_major = lambda st: jnp.swapaxes(st, 1, 2)
    pool_tm, conv_tm, ffn_tm = time_major(state_pool), time_major(state_conv), time_major(state_ffn_conv)
    xs = jnp.swapaxes(x_sample, 0, 1)
    za_s, cin_s, gate_s, v_s = [], [], [], []
    for l in range(depth):
        xs, za, cin, gate, v = _sample_layer(xs, pool_tm, conv_tm, ffn_tm, l, wts,
                                             apply_final_norm=(l == depth - 1))
        za_s.append(za), cin_s.append(cin), gate_s.append(gate), v_s.append(v)
    y_sample = jnp.swapaxes(xs, 0, 1)

    def batch_major(rows):
        return jnp.swapaxes(jnp.stack(rows), 1, 2)

    def shifted_state(state, new_rows):
        new = batch_major(new_rows)
        n_state = state.shape[2]
        if new.shape[2] >= n_state:
            return new[:, :, new.shape[2] - n_state:]
        return jnp.concatenate([state[:, :, new.shape[2]:], new], axis=2)

    return (y_prompt, y_sample,
            jnp.stack(pool_p), shifted_state(state_pool, za_s),
            jnp.stack(conv_p), shifted_state(state_conv, cin_s),
            jnp.stack(ffn_p), shifted_state(state_ffn_conv, gate_s),
            batch_major(v_s))
```

```python
import functools

import jax
import jax.numpy as jnp
from jax import lax
from jax.experimental import pallas as pl
from jax.experimental.pallas import tpu as pltpu

POOL_WINDOWS = (2, 4, 8, 16)
SGU_HEADS = 4
SGU_CHUNK = 128
PAST_LEN = 16384
EPS = 1e-6

LANES = 128
SUBLANES = 8
MXU_DIM = 256
VMEM_LIMIT_BYTES = 60 * 1024 * 1024

PROMPT_ROWS = 512
SAMPLE_SEQS = 64
ROW_BLOCK = 64
POOL_BLOCK = 128
FFN_CHUNK = 4 * MXU_DIM
FFN_ROWS = 128
FFN_LANES = 2 * LANES

F32 = jnp.float32
BF16 = jnp.bfloat16


def _round_up(n, m):
    return -(-n // m) * m


def _rms_scale(x):
    return lax.rsqrt(jnp.mean(x * x, axis=-1, keepdims=True) + EPS)


def _silu(x):
    return x * jax.nn.sigmoid(x)


def _layernorm_silu(y, g, b):
    mu = jnp.mean(y, axis=-1, keepdims=True)
    d = y - mu
    var = jnp.mean(d * d, axis=-1, keepdims=True)
    return _silu(d * lax.rsqrt(var + EPS) * g + b)


def _dot(a, b):
    return jnp.dot(a, b, preferred_element_type=F32)


def _ffn_chunks(d_ff):
    return [(c0, min(FFN_CHUNK, d_ff - c0)) for c0 in range(0, d_ff, FFN_CHUNK)]


def _pool_window_select(snapshots, lane_group):
    out = snapshots[POOL_WINDOWS[0]]
    for g, w in enumerate(POOL_WINDOWS[1:], start=1):
        out = jnp.where(lane_group >= g, snapshots[w], out)
    return out


def _lane_window(lane_group):
    w = jnp.full(lane_group.shape, POOL_WINDOWS[0], jnp.int32)
    for g, win in enumerate(POOL_WINDOWS[1:], start=1):
        w = jnp.where(lane_group >= g, win, w)
    return w


def _prompt_layer_body(
        x_ref, g1_ref, w_in_ref, pool_w_ref, pool_scale_ref, sgu_w_ref, sgu_b_ref,
        conv_w_ref, conv_b_ref, cn_g_ref, cn_b_ref, w_out_ref, g2_ref,
        w_gate_ref, w_val_ref, fconv_w_ref, fconv_b_ref, w_down_ref, fnorm_ref,
        xo_ref, pool_tail_ref, conv_tail_ref, ffn_tail_ref,
        z_buf, za_buf, c_buf, rot_buf, g_halo, g_buf, val_buf, act_buf, mix_buf, acc_buf,
        *, tm, d_a, d_b, d_c, d_ff, pool_halo, conv_halo, ffn_halo, conv_k, ffn_k,
        apply_final_norm):
    s = pl.program_id(1)
    pool_gc = d_a // len(POOL_WINDOWS)
    pool_max = max(POOL_WINDOWS)

    @pl.when(s == 0)
    def _zero_left_state():
        za_buf[0:pool_halo, :] = jnp.zeros((pool_halo, d_a), F32)
        c_buf[0:conv_halo, :] = jnp.zeros((conv_halo, d_c), F32)
        g_halo[...] = jnp.zeros(g_halo.shape, F32)

    x = x_ref[...]
    h = (x * _rms_scale(x) * g1_ref[...]).astype(BF16)
    z_buf[...] = _dot(h, w_in_ref[...])
    o_u, o_v, o_c, o_g = d_a, d_a + d_b, d_a + 2 * d_b, d_a + 2 * d_b + d_c

    za_buf[pool_halo:pool_halo + tm, :] = z_buf[:, 0:d_a]
    lane_group = lax.broadcasted_iota(jnp.int32, (POOL_BLOCK, d_a), 1) // pool_gc
    lane_w = _lane_window(lane_group)
    row = lax.broadcasted_iota(jnp.int32, (POOL_BLOCK, d_a), 0)
    for r0 in range(0, tm, POOL_BLOCK):
        sums = {1: za_buf[r0:r0 + pool_halo + POOL_BLOCK, :]}
        w = 1
        while w < pool_max:
            sums[2 * w] = sums[w] + pltpu.roll(sums[w], w, axis=0)
            w *= 2
        snapshots = {w: sums[w][pool_halo:] for w in POOL_WINDOWS}
        cnt = jnp.minimum(s * tm + r0 + row + 1, lane_w).astype(F32)
        pooled = _pool_window_select(snapshots, lane_group) / cnt - sums[1][pool_halo:]
        mix_buf[r0:r0 + POOL_BLOCK, 0:d_a] = pooled.astype(BF16)
    mixed = _dot(mix_buf[:, 0:d_a], pool_w_ref[...]) * pool_scale_ref[...]
    mix_buf[:, 0:d_a] = mixed.astype(BF16)
    pool_tail_ref[...] = za_buf[tm:tm + pool_halo, :]
    za_buf[0:pool_halo, :] = za_buf[tm:tm + pool_halo, :]

    n_hd = SGU_HEADS * SGU_CHUNK
    w_row = lax.broadcasted_iota(jnp.int32, (n_hd, SGU_CHUNK), 0) % SGU_CHUNK
    w_col = lax.broadcasted_iota(jnp.int32, (n_hd, SGU_CHUNK), 1)
    w_tril = jnp.where(w_col <= w_row, sgu_w_ref[...], 0.0).astype(BF16)
    head_of_lane = lax.broadcasted_iota(jnp.int32, (SGU_CHUNK, d_b), 1) // (d_b // SGU_HEADS)
    for r0 in range(0, tm, SGU_CHUNK):
        v = z_buf[r0:r0 + SGU_CHUNK, o_v:o_c].astype(BF16)
        per_head = _dot(w_tril, v)
        gate = per_head[0:SGU_CHUNK]
        for hd in range(1, SGU_HEADS):
            gate = jnp.where(head_of_lane >= hd,
                             per_head[hd * SGU_CHUNK:(hd + 1) * SGU_CHUNK], gate)
        b_out = z_buf[r0:r0 + SGU_CHUNK, o_u:o_v] * (gate + sgu_b_ref[...])
        mix_buf[r0:r0 + SGU_CHUNK, d_a:d_a + d_b] = b_out.astype(BF16)

    c_buf[conv_halo:conv_halo + tm, :] = z_buf[:, o_c:o_g] * jax.nn.sigmoid(z_buf[:, o_g:o_g + d_c])
    c_all = c_buf[...]
    for r in range(1, SUBLANES):
        rot_buf[r - 1] = pltpu.roll(c_all, r, axis=0)
    for r0 in range(0, tm, ROW_BLOCK):
        acc = jnp.broadcast_to(conv_b_ref[...], (ROW_BLOCK, d_c))
        for k in range(conv_k):
            off = conv_halo - (conv_k - 1) + k
            r = -off % SUBLANES
            lo = off + r + r0
            src = c_buf if r == 0 else rot_buf.at[r - 1]
            acc = acc + conv_w_ref[k:k + 1, :] * src[lo:lo + ROW_BLOCK, :]
        c_out = _layernorm_silu(acc, cn_g_ref[...], cn_b_ref[...])
        mix_buf[r0:r0 + ROW_BLOCK, d_a + d_b:d_a + d_b + d_c] = c_out.astype(BF16)
    conv_tail_ref[...] = c_buf[tm:tm + conv_halo, :]
    c_buf[0:conv_halo, :] = c_buf[tm:tm + conv_halo, :]

    x1 = x + _dot(mix_buf[...], w_out_ref[...])
    acc_buf[...] = x1

    h2 = (x1 * _rms_scale(x1) * g2_ref[...]).astype(BF16)
    for ci, (c0, cw) in enumerate(_ffn_chunks(d_ff)):
        g_cur, val_cur, act_cur = g_buf.at[ci % 2], val_buf.at[ci % 2], act_buf.at[ci % 2]
        g_cur[0:ffn_halo, 0:cw] = g_halo[:, c0:c0 + cw]
        g_cur[ffn_halo:ffn_halo + tm, 0:cw] = _dot(h2, w_gate_ref[:, c0:c0 + cw])
        g_halo[:, c0:c0 + cw] = g_cur[tm:tm + ffn_halo, 0:cw]
        val_cur[:, 0:cw] = _dot(h2, w_val_ref[:, c0:c0 + cw])
        for r0 in range(0, tm, FFN_ROWS):
            for l0 in range(0, cw, FFN_LANES):
                lanes, cols = slice(l0, l0 + FFN_LANES), slice(c0 + l0, c0 + l0 + FFN_LANES)
                g_win = g_cur[r0:r0 + ffn_halo + FFN_ROWS, lanes]
                gate_c = jnp.broadcast_to(fconv_b_ref[:, cols], (FFN_ROWS, FFN_LANES))
                for k in range(ffn_k):
                    shift = ffn_k - 1 - k
                    tap = g_win if shift == 0 else pltpu.roll(g_win, shift, axis=0)
                    gate_c = gate_c + fconv_w_ref[k:k + 1, cols] * tap[ffn_halo:]
                act = _silu(gate_c) * val_cur[r0:r0 + FFN_ROWS, lanes]
                act_cur[r0:r0 + FFN_ROWS, lanes] = act.astype(BF16)
        acc_buf[...] += _dot(act_cur[:, 0:cw], w_down_ref[c0:c0 + cw, :])
    ffn_tail_ref[...] = g_halo[...]

    x2 = acc_buf[...]
    if apply_final_norm:
        x2 = x2 * _rms_scale(x2) * fnorm_ref[...]
    xo_ref[...] = x2


_LAYER_WEIGHTS = ['norm1', 'w_in', 'pool_w', 'pool_scale', 'sgu_w', 'sgu_b', 'conv_w', 'conv_b',
                  'cnorm_g', 'cnorm_b', 'w_out', 'norm2', 'w_gate', 'w_val', 'ffn_conv_w',
                  'ffn_conv_b', 'w_down']


def _resident_spec(arr, layer):
    zeros = (0,) * (arr.ndim - 1)
    return pl.BlockSpec((None,) + arr.shape[1:], lambda *_: (layer,) + zeros,
                        pipeline_mode=pl.Buffered(1))


def _prompt_layer(x, layer, wts, *, apply_final_norm):
    n_b, n_s, d = x.shape
    tm = PROMPT_ROWS
    d_a, d_c, d_ff = wts['pool_w'].shape[-1], wts['conv_w'].shape[-1], wts['w_gate'].shape[-1]
    d_b = wts['sgu_b'].shape[-1]
    conv_k, ffn_k = wts['conv_w'].shape[1], wts['ffn_conv_w'].shape[1]
    pool_halo = _round_up(max(POOL_WINDOWS) - 1, SUBLANES)
    conv_halo = _round_up(conv_k - 1, SUBLANES)
    ffn_halo = _round_up(ffn_k - 1, SUBLANES)
    assert n_s % tm == 0 and tm % SGU_CHUNK == 0 and tm % ROW_BLOCK == 0
    assert tm % POOL_BLOCK == 0 and tm % FFN_ROWS == 0 and d_ff % FFN_LANES == 0
    assert all(w & (w - 1) == 0 for w in POOL_WINDOWS)

    operands = [wts[n] for n in _LAYER_WEIGHTS]
    in_specs = ([pl.BlockSpec((None, tm, d), lambda b, s: (b, s, 0))]
                + [_resident_spec(a, layer) for a in operands]
                + [pl.BlockSpec(wts['final_norm'].shape, lambda b, s: (0, 0),
                                pipeline_mode=pl.Buffered(1))])
    out_shape = [jax.ShapeDtypeStruct((n_b, n_s, d), F32),
                 jax.ShapeDtypeStruct((n_b, pool_halo, d_a), F32),
                 jax.ShapeDtypeStruct((n_b, conv_halo, d_c), F32),
                 jax.ShapeDtypeStruct((n_b, ffn_halo, d_ff), F32)]
    out_specs = [pl.BlockSpec((None, tm, d), lambda b, s: (b, s, 0)),
                 pl.BlockSpec((None, pool_halo, d_a), lambda b, s: (b, 0, 0)),
                 pl.BlockSpec((None, conv_halo, d_c), lambda b, s: (b, 0, 0)),
                 pl.BlockSpec((None, ffn_halo, d_ff), lambda b, s: (b, 0, 0))]
    cw_max = min(FFN_CHUNK, d_ff)
    scratch = [pltpu.VMEM((tm, wts['w_in'].shape[-1]), F32),
               pltpu.VMEM((pool_halo + tm, d_a), F32),
               pltpu.VMEM((conv_halo + tm, d_c), F32),
               pltpu.VMEM((SUBLANES - 1, conv_halo + tm, d_c), F32),
               pltpu.VMEM((ffn_halo, d_ff), F32),
               pltpu.VMEM((2, ffn_halo + tm, cw_max), F32),
               pltpu.VMEM((2, tm, cw_max), F32),
               pltpu.VMEM((2, tm, cw_max), BF16),
               pltpu.VMEM((tm, d), BF16),
               pltpu.VMEM((tm, d), F32)]
    body = functools.partial(
        _prompt_layer_body, tm=tm, d_a=d_a, d_b=d_b, d_c=d_c, d_ff=d_ff,
        pool_halo=pool_halo, conv_halo=conv_halo, ffn_halo=ffn_halo,
        conv_k=conv_k, ffn_k=ffn_k, apply_final_norm=apply_final_norm)
    return pl.pallas_call(
        body, grid=(n_b, n_s // tm), in_specs=in_specs, out_specs=out_specs,
        out_shape=out_shape, scratch_shapes=scratch, name=f'prompt_layer_{layer}',
        compiler_params=pltpu.CompilerParams(
            dimension_semantics=('arbitrary', 'arbitrary'),
            vmem_limit_bytes=VMEM_LIMIT_BYTES),
    )(x, *operands, wts['final_norm'])


def _sample_layer_body(
        x_ref, pool_st_ref, conv_st_ref, ffn_st_ref,
        g1_ref, w_in_ref, pool_w_ref, pool_scale_ref, sgu_w_ref, sgu_b_ref,
        conv_w_ref, conv_b_ref, cn_g_ref, cn_b_ref, w_out_ref, g2_ref,
        w_gate_ref, w_val_ref, fconv_w_ref, fconv_b_ref, w_down_ref, fnorm_ref,
        xo_ref, za_ref, cin_ref, gate_ref, v_ref,
        z_buf, g_buf, mix_buf, acc_buf,
        *, n_t, n_seq, d_a, d_b, d_c, d_ff, conv_k, ffn_k, apply_final_norm):
    pool_gc = d_a // len(POOL_WINDOWS)
    pool_max = max(POOL_WINDOWS)
    n_rows = n_t * n_seq

    def slab(t):
        return slice(t * n_seq, (t + 1) * n_seq)

    x = jnp.concatenate([x_ref[t] for t in range(n_t)], axis=0)
    h = (x * _rms_scale(x) * g1_ref[...]).astype(BF16)
    z_buf[...] = _dot(h, w_in_ref[...])
    o_u, o_v, o_c, o_g = d_a, d_a + d_b, d_a + 2 * d_b, d_a + 2 * d_b + d_c
    for t in range(n_t):
        za_ref[t] = z_buf[slab(t), 0:d_a]
        v_ref[t] = z_buf[slab(t), o_v:o_c]

    n_prev = pool_max - 1
    lane_group = lax.broadcasted_iota(jnp.int32, (n_seq, d_a), 1) // pool_gc
    lane_w = _lane_window(lane_group)

    def pool_row(j):
        return pool_st_ref[j] if j < n_prev else z_buf[slab(j - n_prev), 0:d_a]

    for t in range(n_t):
        cur = pool_row(n_prev + t)
        acc = cur
        snapshots = {}
        for i in range(1, pool_max):
            acc = acc + pool_row(n_prev + t - i)
            if i + 1 in POOL_WINDOWS:
                snapshots[i + 1] = acc
        cnt = jnp.minimum(PAST_LEN + t + 1, lane_w).astype(F32)
        pooled = _pool_window_select(snapshots, lane_group) / cnt - cur
        mix_buf[slab(t), 0:d_a] = pooled.astype(BF16)
    mixed = _dot(mix_buf[:, 0:d_a], pool_w_ref[...]) * pool_scale_ref[...]
    mix_buf[:, 0:d_a] = mixed.astype(BF16)

    for t in range(n_t):
        gate = jnp.broadcast_to(sgu_b_ref[t:t + 1, :], (n_seq, d_b))
        for sp in range(t + 1):
            gate = gate + sgu_w_ref[t * n_t + sp:t * n_t + sp + 1, :] * z_buf[slab(sp), o_v:o_c]
        mix_buf[slab(t), d_a:d_a + d_b] = (z_buf[slab(t), o_u:o_v] * gate).astype(BF16)

    for t in range(n_t):
        cin_ref[t] = z_buf[slab(t), o_c:o_g] * jax.nn.sigmoid(z_buf[slab(t), o_g:o_g + d_c])

    def conv_row(j):
        return conv_st_ref[j] if j < conv_k - 1 else cin_ref[j - (conv_k - 1)]

    for t in range(n_t):
        acc = jnp.broadcast_to(conv_b_ref[...], (n_seq, d_c))
        for k in range(conv_k):
            acc = acc + conv_w_ref[k:k + 1, :] * conv_row(t + k)
        c_out = _layernorm_silu(acc, cn_g_ref[...], cn_b_ref[...])
        mix_buf[slab(t), d_a + d_b:d_a + d_b + d_c] = c_out.astype(BF16)

    x1 = x + _dot(mix_buf[...], w_out_ref[...])
    acc_buf[...] = x1

    h2 = (x1 * _rms_scale(x1) * g2_ref[...]).astype(BF16)
    n_keep = ffn_k - 1
    for c0, cw in _ffn_chunks(d_ff):
        for k in range(n_keep):
            g_buf[slab(k), 0:cw] = ffn_st_ref[k, :, c0:c0 + cw]
        g_buf[n_keep * n_seq:n_keep * n_seq + n_rows, 0:cw] = _dot(h2, w_gate_ref[:, c0:c0 + cw])
        for k in range(n_keep):
            gate_ref[k, :, c0:c0 + cw] = g_buf[slab(n_t + k), 0:cw]
        val = _dot(h2, w_val_ref[:, c0:c0 + cw])
        gate_c = jnp.broadcast_to(fconv_b_ref[:, c0:c0 + cw], (n_rows, cw))
        for k in range(ffn_k):
            gate_c = gate_c + fconv_w_ref[k:k + 1, c0:c0 + cw] * g_buf[k * n_seq:k * n_seq + n_rows, 0:cw]
        act = (_silu(gate_c) * val).astype(BF16)
        acc_buf[...] += _dot(act, w_down_ref[c0:c0 + cw, :])

    x2 = acc_buf[...]
    if apply_final_norm:
        x2 = x2 * _rms_scale(x2) * fnorm_ref[...]
    for t in range(n_t):
        xo_ref[t] = x2[slab(t)]


def _sample_layer(x, pool_st, conv_st, ffn_st, layer, wts, *, apply_final_norm):
    n_t, n_all, d = x.shape
    n_seq = min(SAMPLE_SEQS, n_all)
    d_a, d_c, d_ff = wts['pool_w'].shape[-1], wts['conv_w'].shape[-1], wts['w_gate'].shape[-1]
    d_b = wts['sgu_b'].shape[-1]
    conv_k, ffn_k = wts['conv_w'].shape[1], wts['ffn_conv_w'].shape[1]
    n_keep = ffn_k - 1
    assert n_all % n_seq == 0 and n_seq % SUBLANES == 0
    assert n_keep <= n_t <= min(conv_k - 1, max(POOL_WINDOWS) - 1)

    def state_spec(arr):
        return pl.BlockSpec((None, arr.shape[1], n_seq, arr.shape[3]), lambda i: (layer, 0, i, 0))

    def rows_spec(n_time, ch):
        return pl.BlockSpec((n_time, n_seq, ch), lambda i: (0, i, 0))

    names = [n if n not in ('sgu_w', 'sgu_b') else n + '_new' for n in _LAYER_WEIGHTS]
    operands = [wts[n] for n in names]
    in_specs = ([rows_spec(n_t, d), state_spec(pool_st), state_spec(conv_st), state_spec(ffn_st)]
                + [_resident_spec(a, layer) for a in operands]
                + [pl.BlockSpec(wts['final_norm'].shape, lambda i: (0, 0),
                                pipeline_mode=pl.Buffered(1))])
    out_dims = [(n_t, d), (n_t, d_a), (n_t, d_c), (n_keep, d_ff), (n_t, d_b)]
    out_shape = [jax.ShapeDtypeStruct((nt, n_all, ch), F32) for nt, ch in out_dims]
    out_specs = [rows_spec(nt, ch) for nt, ch in out_dims]
    n_rows = n_t * n_seq
    cw_max = min(FFN_CHUNK, d_ff)
    scratch = [pltpu.VMEM((n_rows, wts['w_in'].shape[-1]), F32),
               pltpu.VMEM((n_keep * n_seq + n_rows, cw_max), F32),
               pltpu.VMEM((n_rows, d), BF16),
               pltpu.VMEM((n_rows, d), F32)]
    body = functools.partial(
        _sample_layer_body, n_t=n_t, n_seq=n_seq, d_a=d_a, d_b=d_b, d_c=d_c, d_ff=d_ff,
        conv_k=conv_k, ffn_k=ffn_k, apply_final_norm=apply_final_norm)
    return pl.pallas_call(
        body, grid=(n_all // n_seq,), in_specs=in_specs, out_specs=out_specs,
        out_shape=out_shape, scratch_shapes=scratch, name=f'sample_layer_{layer}',
        compiler_params=pltpu.CompilerParams(
            dimension_semantics=('arbitrary',), vmem_limit_bytes=VMEM_LIMIT_BYTES),
    )(x, pool_st, conv_st, ffn_st, *operands, wts['final_norm'])


def _block_diag(w):
    depth, n_g, c, _ = w.shape
    eye = jnp.eye(n_g, dtype=w.dtype)
    return jnp.einsum('lgcd,gh->lgchd', w, eye).reshape(depth, n_g * c, n_g * c)


def kernel(x_prompt, x_sample, state_pool, state_conv, state_ffn_conv, norm1, w_in, pool_w, pool_scale, sgu_w, sgu_b, conv_w, conv_b, cnorm_g, cnorm_b, w_out, norm2, w_up, ffn_conv_w, ffn_conv_b, w_down, final_norm):
    depth = norm1.shape[0]
    d_ff = w_down.shape[1]
    d_b = (w_in.shape[-1] - pool_scale.shape[-1] - 2 * conv_w.shape[-1]) // 2
    hd = d_b // SGU_HEADS
    n_t = x_sample.shape[1]

    row = lambda a: a[:, None, :]
    wts = dict(
        norm1=row(norm1), w_in=w_in.astype(BF16),
        pool_w=_block_diag(pool_w).astype(BF16), pool_scale=row(pool_scale),
        sgu_w=sgu_w.reshape(depth, SGU_HEADS * SGU_CHUNK, SGU_CHUNK),
        sgu_b=jnp.repeat(jnp.swapaxes(sgu_b, 1, 2), hd, axis=-1),
        conv_w=conv_w, conv_b=row(conv_b), cnorm_g=row(cnorm_g), cnorm_b=row(cnorm_b),
        w_out=w_out.astype(BF16), norm2=row(norm2),
        w_gate=w_up[:, :, :d_ff].astype(BF16), w_val=w_up[:, :, d_ff:].astype(BF16),
        ffn_conv_w=ffn_conv_w, ffn_conv_b=row(ffn_conv_b), w_down=w_down.astype(BF16),
        final_norm=final_norm[None, :])
    w_new = jnp.transpose(sgu_w[:, :, :n_t, :n_t], (0, 2, 3, 1)).reshape(depth, n_t * n_t, SGU_HEADS)
    wts['sgu_w_new'] = jnp.repeat(w_new, hd, axis=-1)
    wts['sgu_b_new'] = jnp.repeat(jnp.swapaxes(sgu_b[:, :, :n_t], 1, 2), hd, axis=-1)

    x = x_prompt
    pool_p, conv_p, ffn_p = [], [], []
    for l in range(depth):
        x, pt, ct, ft = _prompt_layer(x, l, wts, apply_final_norm=(l == depth - 1))
        pool_p.append(pt[:, pt.shape[1] - (max(POOL_WINDOWS) - 1):])
        conv_p.append(ct[:, ct.shape[1] - (conv_w.shape[1] - 1):])
        ffn_p.append(ft[:, ft.shape[1] - (ffn_conv_w.shape[1] - 1):])
    y_prompt = x

    time_major = lambda st: jnp.swapaxes(st, 1, 2)
    pool_tm, conv_tm, ffn_tm = time_major(state_pool), time_major(state_conv), time_major(state_ffn_conv)
    xs = jnp.swapaxes(x_sample, 0, 1)
    za_s, cin_s, gate_s, v_s = [], [], [], []
    for l in range(depth):
        xs, za, cin, gate, v = _sample_layer(xs, pool_tm, conv_tm, ffn_tm, l, wts,
                                             apply_final_norm=(l == depth - 1))
        za_s.append(za), cin_s.append(cin), gate_s.append(gate), v_s.append(v)
    y_sample = jnp.swapaxes(xs, 0, 1)

    def batch_major(rows):
        return jnp.swapaxes(jnp.stack(rows), 1, 2)

    def shifted_state(state, new_rows):
        new = batch_major(new_rows)
        n_state = state.shape[2]
        if new.shape[2] >= n_state:
            return new[:, :, new.shape[2] - n_state:]
        return jnp.concatenate([state[:, :, new.shape[2]:], new], axis=2)

    return (y_prompt, y_sample,
            jnp.stack(pool_p), shifted_state(state_pool, za_s),
            jnp.stack(conv_p), shifted_state(state_conv, cin_s),
            jnp.stack(ffn_p), shifted_state(state_ffn_conv, gate_s),
            batch_major(v_s))
```

```python
import functools

import jax
import jax.numpy as jnp
from jax import lax
from jax.experimental import pallas as pl
from jax.experimental.pallas import tpu as pltpu

POOL_WINDOWS = (2, 4, 8, 16)
SGU_HEADS = 4
SGU_CHUNK = 128
PAST_LEN = 16384
EPS = 1e-6

LANES = 128
SUBLANES = 8
MXU_DIM = 256
VMEM_LIMIT_BYTES = 60 * 1024 * 1024

PROMPT_ROWS = 512
SAMPLE_SEQS = 64
ROW_BLOCK = 32
POOL_BLOCK = 128
FFN_CHUNK = 4 * MXU_DIM
FFN_UNIT = 2 * MXU_DIM
FFN_ROWS = 128
FFN_LANES = 2 * LANES

F32 = jnp.float32
BF16 = jnp.bfloat16


def _round_up(n, m):
    return -(-n // m) * m


def _rms_scale(x):
    return lax.rsqrt(jnp.mean(x * x, axis=-1, keepdims=True) + EPS)


def _silu(x):
    return x * jax.nn.sigmoid(x)


def _layernorm_silu(y, g, b):
    mu = jnp.mean(y, axis=-1, keepdims=True)
    d = y - mu
    var = jnp.mean(d * d, axis=-1, keepdims=True)
    return _silu(d * lax.rsqrt(var + EPS) * g + b)


def _dot(a, b):
    return jnp.dot(a, b, preferred_element_type=F32)


def _ffn_chunks(d_ff):
    return [(c0, min(FFN_CHUNK, d_ff - c0)) for c0 in range(0, d_ff, FFN_CHUNK)]


def _pool_window_select(snapshots, lane_group):
    out = snapshots[POOL_WINDOWS[0]]
    for g, w in enumerate(POOL_WINDOWS[1:], start=1):
        out = jnp.where(lane_group >= g, snapshots[w], out)
    return out


def _lane_window(lane_group):
    w = jnp.full(lane_group.shape, POOL_WINDOWS[0], jnp.int32)
    for g, win in enumerate(POOL_WINDOWS[1:], start=1):
        w = jnp.where(lane_group >= g, win, w)
    return w


def _ffn_units(d_ff):
    return [(ci, c0, cw, u0, min(FFN_UNIT, cw - u0))
            for ci, (c0, cw) in enumerate(_ffn_chunks(d_ff)) for u0 in range(0, cw, FFN_UNIT)]


def _prompt_layer_body(
        x_ref, g1_ref, w_in_ref, pool_w_ref, pool_scale_ref, sgu_w_ref, sgu_b_ref,
        conv_w_ref, conv_b_ref, cn_g_ref, cn_b_ref, w_out_ref, g2_ref,
        w_gate_ref, w_val_ref, fconv_w_ref, fconv_b_ref, w_down_ref, fnorm_ref,
        xo_ref, pool_tail_ref, conv_tail_ref, ffn_tail_ref,
        z_buf, za_buf, c_buf, rot_buf, wb_buf, g_buf, val_buf, act_buf, mix_buf, x1_buf, h2_buf,
        *, tm, n_tiles, tiles_per_seq, d_a, d_b, d_c, d_ff, pool_halo, conv_halo, ffn_halo,
        conv_k, ffn_k, apply_final_norm):
    j = pl.program_id(0)
    s = lax.rem(jnp.minimum(j, n_tiles - 1), tiles_per_seq)
    o_u, o_v, o_c, o_g = d_a, d_a + d_b, d_a + 2 * d_b, d_a + 2 * d_b + d_c

    @pl.when(j == 0)
    def _no_ffn_tile_yet():
        x1_buf[...] = jnp.zeros(x1_buf.shape, F32)
        h2_buf[...] = jnp.zeros(h2_buf.shape, BF16)
        for k in range(conv_k):
            wb_buf[k * SUBLANES:(k + 1) * SUBLANES, :] = jnp.broadcast_to(conv_w_ref[k:k + 1, :], (SUBLANES, d_c))

    @pl.when(s == 0)
    def _zero_mixer_left_state():
        za_buf[0:pool_halo, :] = jnp.zeros((pool_halo, d_a), F32)
        c_buf[0:conv_halo, :] = jnp.zeros((conv_halo, d_c), F32)

    @pl.when((j == 0) | (lax.rem(j + tiles_per_seq - 1, tiles_per_seq) == 0))
    def _zero_ffn_left_state():
        g_buf[0:ffn_halo, :] = jnp.zeros((ffn_halo, d_ff), F32)

    def mix_project():
        x = x_ref[...]
        h = (x * _rms_scale(x) * g1_ref[...]).astype(BF16)
        z_buf[...] = _dot(h, w_in_ref[...])

    def mix_pool_glu():
        _prompt_pool(z_buf, za_buf, mix_buf, pool_tail_ref, s, tm=tm, d_a=d_a, pool_halo=pool_halo)
        c_buf[conv_halo:conv_halo + tm, :] = z_buf[:, o_c:o_g] * jax.nn.sigmoid(z_buf[:, o_g:o_g + d_c])

    def mix_sgu():
        n_hd = SGU_HEADS * SGU_CHUNK
        w_row = lax.broadcasted_iota(jnp.int32, (n_hd, SGU_CHUNK), 0) % SGU_CHUNK
        w_col = lax.broadcasted_iota(jnp.int32, (n_hd, SGU_CHUNK), 1)
        w_tril = jnp.where(w_col <= w_row, sgu_w_ref[...], 0.0).astype(BF16)
        head_of_lane = lax.broadcasted_iota(jnp.int32, (SGU_CHUNK, d_b), 1) // (d_b // SGU_HEADS)
        for r0 in range(0, tm, SGU_CHUNK):
            v = z_buf[r0:r0 + SGU_CHUNK, o_v:o_c].astype(BF16)
            per_head = _dot(w_tril, v)
            gate = per_head[0:SGU_CHUNK]
            for hd in range(1, SGU_HEADS):
                gate = jnp.where(head_of_lane >= hd,
                                 per_head[hd * SGU_CHUNK:(hd + 1) * SGU_CHUNK], gate)
            b_out = z_buf[r0:r0 + SGU_CHUNK, o_u:o_v] * (gate + sgu_b_ref[...])
            mix_buf[r0:r0 + SGU_CHUNK, d_a:d_a + d_b] = b_out.astype(BF16)

    def mix_conv(row_lo, row_hi):
        window = c_buf[row_lo:row_hi + conv_halo, :]
        for r in range(1, SUBLANES):
            rot_buf[r - 1] = pltpu.roll(window, r, axis=0)
        for r0 in range(0, row_hi - row_lo, ROW_BLOCK):
            acc = jnp.broadcast_to(conv_b_ref[...], (ROW_BLOCK, d_c))
            for k in range(conv_k):
                off = conv_halo - (conv_k - 1) + k
                r = -off % SUBLANES
                lo = off + r + r0
                tap = (c_buf[row_lo + lo:row_lo + lo + ROW_BLOCK, :] if r == 0
                       else rot_buf[r - 1, lo:lo + ROW_BLOCK, :])
                w_k = wb_buf[k * SUBLANES:(k + 1) * SUBLANES, :]
                acc = acc + jnp.concatenate([w_k] * (ROW_BLOCK // SUBLANES), axis=0) * tap
            c_out = _layernorm_silu(acc, cn_g_ref[...], cn_b_ref[...])
            mix_buf[row_lo + r0:row_lo + r0 + ROW_BLOCK, d_a + d_b:d_a + d_b + d_c] = c_out.astype(BF16)

    def mix_output():
        conv_tail_ref[...] = c_buf[tm:tm + conv_halo, :]
        c_buf[0:conv_halo, :] = c_buf[tm:tm + conv_halo, :]
        mixed = _dot(mix_buf[:, 0:d_a], pool_w_ref[...]) * pool_scale_ref[...]
        mix_buf[:, 0:d_a] = mixed.astype(BF16)
        x1 = x_ref[...] + _dot(mix_buf[...], w_out_ref[...])
        x1_buf[...] = x1
        h2_buf[...] = (x1 * _rms_scale(x1) * g2_ref[...]).astype(BF16)

    def ffn_dots(unit):
        _, c0, _, u0, uw = unit
        cols = slice(c0 + u0, c0 + u0 + uw)
        g_buf[ffn_halo:ffn_halo + tm, cols] = _dot(h2_buf[...], w_gate_ref[:, cols])
        val_buf[:, cols] = _dot(h2_buf[...], w_val_ref[:, cols])

    def ffn_gate(unit):
        ci, c0, _, u0, uw = unit
        act_cur = act_buf.at[ci % 2]
        for r0 in range(0, tm, FFN_ROWS):
            for l0 in range(0, uw, FFN_LANES):
                cols = slice(c0 + u0 + l0, c0 + u0 + l0 + FFN_LANES)
                g_win = g_buf[r0:r0 + ffn_halo + FFN_ROWS, cols]
                gate_c = jnp.broadcast_to(fconv_b_ref[:, cols], (FFN_ROWS, FFN_LANES))
                for k in range(ffn_k):
                    shift = ffn_k - 1 - k
                    tap = g_win if shift == 0 else pltpu.roll(g_win, shift, axis=0)
                    gate_c = gate_c + fconv_w_ref[k:k + 1, cols] * tap[ffn_halo:]
                act = _silu(gate_c) * val_buf[r0:r0 + FFN_ROWS, cols]
                act_cur[r0:r0 + FFN_ROWS, u0 + l0:u0 + l0 + FFN_LANES] = act.astype(BF16)
        cols = slice(c0 + u0, c0 + u0 + uw)
        ffn_tail_ref[:, cols] = g_buf[tm:tm + ffn_halo, cols]
        g_buf[0:ffn_halo, cols] = g_buf[tm:tm + ffn_halo, cols]

    chunks = _ffn_chunks(d_ff)

    def ffn_down(ci):
        c0, cw = chunks[ci]
        part = _dot(act_buf.at[ci % 2][:, 0:cw], w_down_ref[c0:c0 + cw, :])
        if ci == 0:
            xo_ref[...] = x1_buf[...] + part
        elif ci < len(chunks) - 1:
            xo_ref[...] += part
        else:
            x2 = xo_ref[...] + part
            if apply_final_norm:
                x2 = x2 * _rms_scale(x2) * fnorm_ref[...]
            xo_ref[...] = x2

    units = _ffn_units(d_ff)
    assert [u[0] for u in units] == [0, 0, 1, 1, 2, 2]
    ffn_dots(units[0])
    mix_project()
    ffn_dots(units[1])
    ffn_dots(units[2])
    ffn_dots(units[3])
    mix_pool_glu()
    ffn_gate(units[0])
    ffn_gate(units[1])
    mix_sgu()
    ffn_dots(units[4])
    ffn_down(0)
    mix_conv(0, tm // 2)
    mix_conv(tm // 2, tm)
    ffn_gate(units[2])
    ffn_gate(units[3])
    ffn_dots(units[5])
    ffn_down(1)
    mix_output()
    ffn_gate(units[4])
    ffn_gate(units[5])
    ffn_down(2)


def _prompt_pool(z_buf, za_buf, mix_buf, pool_tail_ref, s, *, tm, d_a, pool_halo):
    pool_gc = d_a // len(POOL_WINDOWS)
    pool_max = max(POOL_WINDOWS)
    za_buf[pool_halo:pool_halo + tm, :] = z_buf[:, 0:d_a]
    lane_group = lax.broadcasted_iota(jnp.int32, (POOL_BLOCK, d_a), 1) // pool_gc
    lane_w = _lane_window(lane_group)
    row = lax.broadcasted_iota(jnp.int32, (POOL_BLOCK, d_a), 0)
    for r0 in range(0, tm, POOL_BLOCK):
        sums = {1: za_buf[r0:r0 + pool_halo + POOL_BLOCK, :]}
        w = 1
        while w < pool_max:
            sums[2 * w] = sums[w] + pltpu.roll(sums[w], w, axis=0)
            w *= 2
        snapshots = {w: sums[w][pool_halo:] for w in POOL_WINDOWS}
        cnt = jnp.minimum(s * tm + r0 + row + 1, lane_w).astype(F32)
        pooled = _pool_window_select(snapshots, lane_group) / cnt - sums[1][pool_halo:]
        mix_buf[r0:r0 + POOL_BLOCK, 0:d_a] = pooled.astype(BF16)
    pool_tail_ref[...] = za_buf[tm:tm + pool_halo, :]
    za_buf[0:pool_halo, :] = za_buf[tm:tm + pool_halo, :]


_LAYER_WEIGHTS = ['norm1', 'w_in', 'pool_w', 'pool_scale', 'sgu_w', 'sgu_b', 'conv_w', 'conv_b',
                  'cnorm_g', 'cnorm_b', 'w_out', 'norm2', 'w_gate', 'w_val', 'ffn_conv_w',
                  'ffn_conv_b', 'w_down']


_UP_HALF = {'w_gate': 0, 'w_val': 1}


def _layer_operands(wts, names):
    return [wts['w_up'] if n in _UP_HALF else wts[n] for n in names]


def _resident_spec(name, arr, layer):
    shape, index = arr.shape[1:], (0,) * (arr.ndim - 1)
    if name in _UP_HALF:
        shape, index = shape[:-1] + (shape[-1] // 2,), index[:-1] + (_UP_HALF[name],)
    return pl.BlockSpec((None,) + shape, lambda *_: (layer,) + index, pipeline_mode=pl.Buffered(1))


def _prompt_layer(x, layer, wts, *, apply_final_norm):
    n_b, n_s, d = x.shape
    tm = PROMPT_ROWS
    d_a, d_c, d_ff = wts['pool_w'].shape[-1], wts['conv_w'].shape[-1], wts['w_down'].shape[1]
    d_b = wts['sgu_b'].shape[-1]
    conv_k, ffn_k = wts['conv_w'].shape[1], wts['ffn_conv_w'].shape[1]
    pool_halo = _round_up(max(POOL_WINDOWS) - 1, SUBLANES)
    conv_halo = _round_up(conv_k - 1, SUBLANES)
    ffn_halo = _round_up(ffn_k - 1, SUBLANES)
    assert n_s % tm == 0 and tm % SGU_CHUNK == 0 and (tm // 2) % ROW_BLOCK == 0
    assert tm % POOL_BLOCK == 0 and tm % FFN_ROWS == 0 and d_ff % FFN_LANES == 0
    assert all(w & (w - 1) == 0 for w in POOL_WINDOWS)

    tiles_per_seq = n_s // tm
    n_tiles = n_b * tiles_per_seq

    def mixer_tile(j):
        t = jnp.minimum(j, n_tiles - 1)
        return t // tiles_per_seq, t % tiles_per_seq

    def ffn_tile(j):
        t = jnp.maximum(j - 1, 0)
        return t // tiles_per_seq, t % tiles_per_seq

    operands = _layer_operands(wts, _LAYER_WEIGHTS)
    in_specs = ([pl.BlockSpec((None, tm, d), lambda j: (*mixer_tile(j), 0))]
                + [_resident_spec(n, a, layer) for n, a in zip(_LAYER_WEIGHTS, operands)]
                + [pl.BlockSpec(wts['final_norm'].shape, lambda j: (0, 0),
                                pipeline_mode=pl.Buffered(1))])
    out_shape = [jax.ShapeDtypeStruct((n_b, n_s, d), F32),
                 jax.ShapeDtypeStruct((n_b, pool_halo, d_a), F32),
                 jax.ShapeDtypeStruct((n_b, conv_halo, d_c), F32),
                 jax.ShapeDtypeStruct((n_b, ffn_halo, d_ff), F32)]
    out_specs = [pl.BlockSpec((None, tm, d), lambda j: (*ffn_tile(j), 0)),
                 pl.BlockSpec((None, pool_halo, d_a), lambda j: (mixer_tile(j)[0], 0, 0)),
                 pl.BlockSpec((None, conv_halo, d_c), lambda j: (mixer_tile(j)[0], 0, 0)),
                 pl.BlockSpec((None, ffn_halo, d_ff), lambda j: (ffn_tile(j)[0], 0, 0))]
    cw_max = min(FFN_CHUNK, d_ff)
    scratch = [pltpu.VMEM((tm, wts['w_in'].shape[-1]), F32),
               pltpu.VMEM((pool_halo + tm, d_a), F32),
               pltpu.VMEM((conv_halo + tm, d_c), F32),
               pltpu.VMEM((SUBLANES - 1, conv_halo + tm // 2, d_c), F32),
               pltpu.VMEM((conv_k * SUBLANES, d_c), F32),
               pltpu.VMEM((ffn_halo + tm, d_ff), F32),
               pltpu.VMEM((tm, d_ff), F32),
               pltpu.VMEM((2, tm, cw_max), BF16),
               pltpu.VMEM((tm, d), BF16),
               pltpu.VMEM((tm, d), F32),
               pltpu.VMEM((tm, d), BF16)]
    body = functools.partial(
        _prompt_layer_body, tm=tm, n_tiles=n_tiles, tiles_per_seq=tiles_per_seq,
        d_a=d_a, d_b=d_b, d_c=d_c, d_ff=d_ff,
        pool_halo=pool_halo, conv_halo=conv_halo, ffn_halo=ffn_halo,
        conv_k=conv_k, ffn_k=ffn_k, apply_final_norm=apply_final_norm)
    return pl.pallas_call(
        body, grid=(n_tiles + 1,), in_specs=in_specs, out_specs=out_specs,
        out_shape=out_shape, scratch_shapes=scratch, name=f'prompt_layer_{layer}',
        compiler_params=pltpu.CompilerParams(
            dimension_semantics=('arbitrary',), vmem_limit_bytes=VMEM_LIMIT_BYTES),
    )(x, *operands, wts['final_norm'])


def _sample_layer_body(
        x_ref, pool_st_ref, conv_st_ref, ffn_st_ref,
        g1_ref, w_in_ref, pool_w_ref, pool_scale_ref, sgu_w_ref, sgu_b_ref,
        conv_w_ref, conv_b_ref, cn_g_ref, cn_b_ref, w_out_ref, g2_ref,
        w_gate_ref, w_val_ref, fconv_w_ref, fconv_b_ref, w_down_ref, fnorm_ref,
        xo_ref, za_ref, cin_ref, gate_ref, v_ref,
        z_buf, g_buf, mix_buf, acc_buf,
        *, n_t, n_seq, d_a, d_b, d_c, d_ff, conv_k, ffn_k, apply_final_norm):
    pool_gc = d_a // len(POOL_WINDOWS)
    pool_max = max(POOL_WINDOWS)
    n_rows = n_t * n_seq

    def slab(t):
        return slice(t * n_seq, (t + 1) * n_seq)

    x = jnp.concatenate([x_ref[t] for t in range(n_t)], axis=0)
    h = (x * _rms_scale(x) * g1_ref[...]).astype(BF16)
    z_buf[...] = _dot(h, w_in_ref[...])
    o_u, o_v, o_c, o_g = d_a, d_a + d_b, d_a + 2 * d_b, d_a + 2 * d_b + d_c
    for t in range(n_t):
        za_ref[t] = z_buf[slab(t), 0:d_a]
        v_ref[t] = z_buf[slab(t), o_v:o_c]

    n_prev = pool_max - 1
    lane_group = lax.broadcasted_iota(jnp.int32, (n_seq, d_a), 1) // pool_gc
    lane_w = _lane_window(lane_group)

    def pool_row(j):
        return pool_st_ref[j] if j < n_prev else z_buf[slab(j - n_prev), 0:d_a]

    for t in range(n_t):
        cur = pool_row(n_prev + t)
        acc = cur
        snapshots = {}
        for i in range(1, pool_max):
            acc = acc + pool_row(n_prev + t - i)
            if i + 1 in POOL_WINDOWS:
                snapshots[i + 1] = acc
        cnt = jnp.minimum(PAST_LEN + t + 1, lane_w).astype(F32)
        pooled = _pool_window_select(snapshots, lane_group) / cnt - cur
        mix_buf[slab(t), 0:d_a] = pooled.astype(BF16)
    mixed = _dot(mix_buf[:, 0:d_a], pool_w_ref[...]) * pool_scale_ref[...]
    mix_buf[:, 0:d_a] = mixed.astype(BF16)

    for t in range(n_t):
        gate = jnp.broadcast_to(sgu_b_ref[t:t + 1, :], (n_seq, d_b))
        for sp in range(t + 1):
            gate = gate + sgu_w_ref[t * n_t + sp:t * n_t + sp + 1, :] * z_buf[slab(sp), o_v:o_c]
        mix_buf[slab(t), d_a:d_a + d_b] = (z_buf[slab(t), o_u:o_v] * gate).astype(BF16)

    for t in range(n_t):
        cin_ref[t] = z_buf[slab(t), o_c:o_g] * jax.nn.sigmoid(z_buf[slab(t), o_g:o_g + d_c])

    def conv_row(j):
        return conv_st_ref[j] if j < conv_k - 1 else cin_ref[j - (conv_k - 1)]

    for t in range(n_t):
        acc = jnp.broadcast_to(conv_b_ref[...], (n_seq, d_c))
        for k in range(conv_k):
            acc = acc + conv_w_ref[k:k + 1, :] * conv_row(t + k)
        c_out = _layernorm_silu(acc, cn_g_ref[...], cn_b_ref[...])
        mix_buf[slab(t), d_a + d_b:d_a + d_b + d_c] = c_out.astype(BF16)

    x1 = x + _dot(mix_buf[...], w_out_ref[...])
    acc_buf[...] = x1

    h2 = (x1 * _rms_scale(x1) * g2_ref[...]).astype(BF16)
    n_keep = ffn_k - 1
    for c0, cw in _ffn_chunks(d_ff):
        for k in range(n_keep):
            g_buf[slab(k), 0:cw] = ffn_st_ref[k, :, c0:c0 + cw]
        g_buf[n_keep * n_seq:n_keep * n_seq + n_rows, 0:cw] = _dot(h2, w_gate_ref[:, c0:c0 + cw])
        for k in range(n_keep):
            gate_ref[k, :, c0:c0 + cw] = g_buf[slab(n_t + k), 0:cw]
        val = _dot(h2, w_val_ref[:, c0:c0 + cw])
        gate_c = jnp.broadcast_to(fconv_b_ref[:, c0:c0 + cw], (n_rows, cw))
        for k in range(ffn_k):
            gate_c = gate_c + fconv_w_ref[k:k + 1, c0:c0 + cw] * g_buf[k * n_seq:k * n_seq + n_rows, 0:cw]
        act = (_silu(gate_c) * val).astype(BF16)
        acc_buf[...] += _dot(act, w_down_ref[c0:c0 + cw, :])

    x2 = acc_buf[...]
    if apply_final_norm:
        x2 = x2 * _rms_scale(x2) * fnorm_ref[...]
    for t in range(n_t):
        xo_ref[t] = x2[slab(t)]


def _sample_layer(x, pool_st, conv_st, ffn_st, layer, wts, *, apply_final_norm):
    n_t, n_all, d = x.shape
    n_seq = min(SAMPLE_SEQS, n_all)
    d_a, d_c, d_ff = wts['pool_w'].shape[-1], wts['conv_w'].shape[-1], wts['w_down'].shape[1]
    d_b = wts['sgu_b'].shape[-1]
    conv_k, ffn_k = wts['conv_w'].shape[1], wts['ffn_conv_w'].shape[1]
    n_keep = ffn_k - 1
    assert n_all % n_seq == 0 and n_seq % SUBLANES == 0
    assert n_keep <= n_t <= min(conv_k - 1, max(POOL_WINDOWS) - 1)

    def state_spec(arr):
        return pl.BlockSpec((None, arr.shape[1], n_seq, arr.shape[3]), lambda i: (layer, 0, i, 0))

    def rows_spec(n_time, ch):
        return pl.BlockSpec((n_time, n_seq, ch), lambda i: (0, i, 0))

    names = [n if n not in ('sgu_w', 'sgu_b') else n + '_new' for n in _LAYER_WEIGHTS]
    operands = _layer_operands(wts, names)
    in_specs = ([rows_spec(n_t, d), state_spec(pool_st), state_spec(conv_st), state_spec(ffn_st)]
                + [_resident_spec(n, a, layer) for n, a in zip(names, operands)]
                + [pl.BlockSpec(wts['final_norm'].shape, lambda i: (0, 0),
                                pipeline_mode=pl.Buffered(1))])
    out_dims = [(n_t, d), (n_t, d_a), (n_t, d_c), (n_keep, d_ff), (n_t, d_b)]
    out_shape = [jax.ShapeDtypeStruct((nt, n_all, ch), F32) for nt, ch in out_dims]
    out_specs = [rows_spec(nt, ch) for nt, ch in out_dims]
    n_rows = n_t * n_seq
    cw_max = min(FFN_CHUNK, d_ff)
    scratch = [pltpu.VMEM((n_rows, wts['w_in'].shape[-1]), F32),
               pltpu.VMEM((n_keep * n_seq + n_rows, cw_max), F32),
               pltpu.VMEM((n_rows, d), BF16),
               pltpu.VMEM((n_rows, d), F32)]
    body = functools.partial(
        _sample_layer_body, n_t=n_t, n_seq=n_seq, d_a=d_a, d_b=d_b, d_c=d_c, d_ff=d_ff,
        conv_k=conv_k, ffn_k=ffn_k, apply_final_norm=apply_final_norm)
    return pl.pallas_call(
        body, grid=(n_all // n_seq,), in_specs=in_specs, out_specs=out_specs,
        out_shape=out_shape, scratch_shapes=scratch, name=f'sample_layer_{layer}',
        compiler_params=pltpu.CompilerParams(
            dimension_semantics=('arbitrary',), vmem_limit_bytes=VMEM_LIMIT_BYTES),
    )(x, pool_st, conv_st, ffn_st, *operands, wts['final_norm'])


def _block_diag(w):
    depth, n_g, c, _ = w.shape
    eye = jnp.eye(n_g, dtype=w.dtype)
    return jnp.einsum('lgcd,gh->lgchd', w, eye).reshape(depth, n_g * c, n_g * c)


def kernel(x_prompt, x_sample, state_pool, state_conv, state_ffn_conv, norm1, w_in, pool_w, pool_scale, sgu_w, sgu_b, conv_w, conv_b, cnorm_g, cnorm_b, w_out, norm2, w_up, ffn_conv_w, ffn_conv_b, w_down, final_norm):
    depth = norm1.shape[0]
    d_ff = w_down.shape[1]
    d_b = (w_in.shape[-1] - pool_scale.shape[-1] - 2 * conv_w.shape[-1]) // 2
    hd = d_b // SGU_HEADS
    n_t = x_sample.shape[1]

    row = lambda a: a[:, None, :]
    wts = dict(
        norm1=row(norm1), w_in=w_in.astype(BF16),
        pool_w=_block_diag(pool_w).astype(BF16), pool_scale=row(pool_scale),
        sgu_w=sgu_w.reshape(depth, SGU_HEADS * SGU_CHUNK, SGU_CHUNK),
        sgu_b=jnp.repeat(jnp.swapaxes(sgu_b, 1, 2), hd, axis=-1),
        conv_w=conv_w, conv_b=row(conv_b), cnorm_g=row(cnorm_g), cnorm_b=row(cnorm_b),
        w_out=w_out.astype(BF16), norm2=row(norm2),
        w_up=w_up.astype(BF16),
        ffn_conv_w=ffn_conv_w, ffn_conv_b=row(ffn_conv_b), w_down=w_down.astype(BF16),
        final_norm=final_norm[None, :])
    w_new = jnp.transpose(sgu_w[:, :, :n_t, :n_t], (0, 2, 3, 1)).reshape(depth, n_t * n_t, SGU_HEADS)
    wts['sgu_w_new'] = jnp.repeat(w_new, hd, axis=-1)
    wts['sgu_b_new'] = jnp.repeat(jnp.swapaxes(sgu_b[:, :, :n_t], 1, 2), hd, axis=-1)

    x = x_prompt
    pool_p, conv_p, ffn_p = [], [], []
    for l in range(depth):
        x, pt, ct, ft = _prompt_layer(x, l, wts, apply_final_norm=(l == depth - 1))
        pool_p.append(pt[:, pt.shape[1] - (max(POOL_WINDOWS) - 1):])
        conv_p.append(ct[:, ct.shape[1] - (conv_w.shape[1] - 1):])
        ffn_p.append(ft[:, ft.shape[1] - (ffn_conv_w.shape[1] - 1):])
    y_prompt = x

    time_major = lambda st: jnp.swapaxes(st, 1, 2)
    pool_tm, conv_tm, ffn_tm = time_major(state_pool), time_major(state_conv), time_major(state_ffn_conv)
    xs = jnp.swapaxes(x_sample, 0, 1)
    za_s, cin_s, gate_s, v_s = [], [], [], []
    for l in range(depth):
        xs, za, cin, gate, v = _sample_layer(xs, pool_tm, conv_tm, ffn_tm, l, wts,
                                             apply_final_norm=(l == depth - 1))
        za_s.append(za), cin_s.append(cin), gate_s.append(gate), v_s.append(v)
    y_sample = jnp.swapaxes(xs, 0, 1)

    def batch_major(rows):
        return jnp.swapaxes(jnp.stack(rows), 1, 2)

    def shifted_state(state, new_rows):
        new = batch_major(new_rows)
        n_state = state.shape[2]
        if new.shape[2] >= n_state:
            return new[:, :, new.shape[2] - n_state:]
        return jnp.concatenate([state[:, :, new.shape[2]:], new], axis=2)

    return (y_prompt, y_sample,
            jnp.stack(pool_p), shifted_state(state_pool, za_s),
            jnp.stack(conv_p), shifted_state(state_conv, cin_s),
            jnp.stack(ffn_p), shifted_state(state_ffn_conv, gate_s),
            batch_major(v_s))
```

```python
import functools

import jax
import jax.numpy as jnp
from jax import lax
from jax.experimental import pallas as pl
from jax.experimental.pallas import tpu as pltpu

POOL_WINDOWS = (2, 4, 8, 16)
SGU_HEADS = 4
SGU_CHUNK = 128
PAST_LEN = 16384
EPS = 1e-6

LANES = 128
SUBLANES = 8
MXU_DIM = 256
VMEM_LIMIT_BYTES = 60 * 1024 * 1024

PROMPT_ROWS = 512
SAMPLE_SEQS = 64
ROW_BLOCK = 32
POOL_BLOCK = 128
FFN_CHUNK = 4 * MXU_DIM
FFN_UNIT = 2 * MXU_DIM
FFN_ROWS = 128
FFN_LANES = 2 * LANES

F32 = jnp.float32
BF16 = jnp.bfloat16


def _round_up(n, m):
    return -(-n // m) * m


def _rms_scale(x):
    return lax.rsqrt(jnp.mean(x * x, axis=-1, keepdims=True) + EPS)


def _silu(x):
    return x * jax.nn.sigmoid(x)


def _layernorm_silu(y, g, b):
    mu = jnp.mean(y, axis=-1, keepdims=True)
    d = y - mu
    var = jnp.mean(d * d, axis=-1, keepdims=True)
    return _silu(d * lax.rsqrt(var + EPS) * g + b)


def _dot(a, b):
    return jnp.dot(a, b, preferred_element_type=F32)


def _ffn_chunks(d_ff):
    return [(c0, min(FFN_CHUNK, d_ff - c0)) for c0 in range(0, d_ff, FFN_CHUNK)]


def _pool_window_select(snapshots, lane_group):
    out = snapshots[POOL_WINDOWS[0]]
    for g, w in enumerate(POOL_WINDOWS[1:], start=1):
        out = jnp.where(lane_group >= g, snapshots[w], out)
    return out


def _lane_window(lane_group):
    w = jnp.full(lane_group.shape, POOL_WINDOWS[0], jnp.int32)
    for g, win in enumerate(POOL_WINDOWS[1:], start=1):
        w = jnp.where(lane_group >= g, win, w)
    return w


def _ffn_units(d_ff):
    return [(ci, c0, cw, u0, min(FFN_UNIT, cw - u0))
            for ci, (c0, cw) in enumerate(_ffn_chunks(d_ff)) for u0 in range(0, cw, FFN_UNIT)]


def _prompt_layer_body(
        x_ref, g1_ref, w_in_ref, pool_w_ref, pool_scale_ref, sgu_w_ref, sgu_b_ref,
        conv_w_ref, conv_b_ref, cn_g_ref, cn_b_ref, w_out_ref, g2_ref,
        w_gate_ref, w_val_ref, fconv_w_ref, fconv_b_ref, w_down_ref, fnorm_ref,
        xo_ref, pool_tail_ref, conv_tail_ref, ffn_tail_ref,
        z_buf, za_buf, c_buf, rot_buf, wb_buf, g_buf, val_buf, act_buf, mix_buf, x1_buf, h2_buf,
        *, tm, n_tiles, tiles_per_seq, d_a, d_b, d_c, d_ff, pool_halo, conv_halo, ffn_halo,
        conv_k, ffn_k, apply_final_norm):
    j = pl.program_id(0)
    s = lax.rem(jnp.minimum(j, n_tiles - 1), tiles_per_seq)
    o_u, o_v, o_c, o_g = d_a, d_a + d_b, d_a + 2 * d_b, d_a + 2 * d_b + d_c

    @pl.when(j == 0)
    def _no_ffn_tile_yet():
        x1_buf[...] = jnp.zeros(x1_buf.shape, F32)
        h2_buf[...] = jnp.zeros(h2_buf.shape, BF16)
        for k in range(conv_k):
            wb_buf[k * SUBLANES:(k + 1) * SUBLANES, :] = jnp.broadcast_to(conv_w_ref[k:k + 1, :], (SUBLANES, d_c))

    @pl.when(s == 0)
    def _zero_mixer_left_state():
        za_buf[0:pool_halo, :] = jnp.zeros((pool_halo, d_a), F32)
        c_buf[0:conv_halo, :] = jnp.zeros((conv_halo, d_c), F32)

    @pl.when((j == 0) | (lax.rem(j + tiles_per_seq - 1, tiles_per_seq) == 0))
    def _zero_ffn_left_state():
        g_buf[0:ffn_halo, :] = jnp.zeros((ffn_halo, d_ff), F32)

    def mix_project():
        x = x_ref[...]
        h = (x * _rms_scale(x) * g1_ref[...]).astype(BF16)
        z_buf[...] = _dot(h, w_in_ref[...])

    def mix_pool_glu():
        _prompt_pool(z_buf, za_buf, mix_buf, pool_tail_ref, s, tm=tm, d_a=d_a, pool_halo=pool_halo)
        c_buf[conv_halo:conv_halo + tm, :] = z_buf[:, o_c:o_g] * jax.nn.sigmoid(z_buf[:, o_g:o_g + d_c])

    def mix_sgu():
        n_hd = SGU_HEADS * SGU_CHUNK
        w_row = lax.broadcasted_iota(jnp.int32, (n_hd, SGU_CHUNK), 0) % SGU_CHUNK
        w_col = lax.broadcasted_iota(jnp.int32, (n_hd, SGU_CHUNK), 1)
        w_tril = jnp.where(w_col <= w_row, sgu_w_ref[...], 0.0).astype(BF16)
        head_of_lane = lax.broadcasted_iota(jnp.int32, (SGU_CHUNK, d_b), 1) // (d_b // SGU_HEADS)
        for r0 in range(0, tm, SGU_CHUNK):
            v = z_buf[r0:r0 + SGU_CHUNK, o_v:o_c].astype(BF16)
            per_head = _dot(w_tril, v)
            gate = per_head[0:SGU_CHUNK]
            for hd in range(1, SGU_HEADS):
                gate = jnp.where(head_of_lane >= hd,
                                 per_head[hd * SGU_CHUNK:(hd + 1) * SGU_CHUNK], gate)
            b_out = z_buf[r0:r0 + SGU_CHUNK, o_u:o_v] * (gate + sgu_b_ref[...])
            mix_buf[r0:r0 + SGU_CHUNK, d_a:d_a + d_b] = b_out.astype(BF16)

    def mix_conv(row_lo, row_hi):
        window = c_buf[row_lo:row_hi + conv_halo, :]
        for r in range(1, SUBLANES):
            rot_buf[r - 1] = pltpu.roll(window, r, axis=0)
        for r0 in range(0, row_hi - row_lo, ROW_BLOCK):
            acc = jnp.broadcast_to(conv_b_ref[...], (ROW_BLOCK, d_c))
            for k in range(conv_k):
                off = conv_halo - (conv_k - 1) + k
                r = -off % SUBLANES
                lo = off + r + r0
                tap = (c_buf[row_lo + lo:row_lo + lo + ROW_BLOCK, :] if r == 0
                       else rot_buf[r - 1, lo:lo + ROW_BLOCK, :])
                w_k = wb_buf[k * SUBLANES:(k + 1) * SUBLANES, :]
                acc = acc + jnp.concatenate([w_k] * (ROW_BLOCK // SUBLANES), axis=0) * tap
            c_out = _layernorm_silu(acc, cn_g_ref[...], cn_b_ref[...])
            mix_buf[row_lo + r0:row_lo + r0 + ROW_BLOCK, d_a + d_b:d_a + d_b + d_c] = c_out.astype(BF16)

    def mix_output():
        conv_tail_ref[...] = c_buf[tm:tm + conv_halo, :]
        c_buf[0:conv_halo, :] = c_buf[tm:tm + conv_halo, :]
        mixed = _dot(mix_buf[:, 0:d_a], pool_w_ref[...]) * pool_scale_ref[...]
        mix_buf[:, 0:d_a] = mixed.astype(BF16)
        x1 = x_ref[...] + _dot(mix_buf[...], w_out_ref[...])
        x1_buf[...] = x1
        h2_buf[...] = (x1 * _rms_scale(x1) * g2_ref[...]).astype(BF16)

    def ffn_dots(unit):
        _, c0, _, u0, uw = unit
        cols = slice(c0 + u0, c0 + u0 + uw)
        g_buf[ffn_halo:ffn_halo + tm, cols] = _dot(h2_buf[...], w_gate_ref[:, cols])
        val_buf[:, cols] = _dot(h2_buf[...], w_val_ref[:, cols])

    def ffn_gate(unit):
        ci, c0, _, u0, uw = unit
        act_cur = act_buf.at[ci % 2]
        for r0 in range(0, tm, FFN_ROWS):
            for l0 in range(0, uw, FFN_LANES):
                cols = slice(c0 + u0 + l0, c0 + u0 + l0 + FFN_LANES)
                g_win = g_buf[r0:r0 + ffn_halo + FFN_ROWS, cols]
                gate_c = jnp.broadcast_to(fconv_b_ref[:, cols], (FFN_ROWS, FFN_LANES))
                for k in range(ffn_k):
                    shift = ffn_k - 1 - k
                    tap = g_win if shift == 0 else pltpu.roll(g_win, shift, axis=0)
                    gate_c = gate_c + fconv_w_ref[k:k + 1, cols] * tap[ffn_halo:]
                act = _silu(gate_c) * val_buf[r0:r0 + FFN_ROWS, cols]
                act_cur[r0:r0 + FFN_ROWS, u0 + l0:u0 + l0 + FFN_LANES] = act.astype(BF16)
        cols = slice(c0 + u0, c0 + u0 + uw)
        ffn_tail_ref[:, cols] = g_buf[tm:tm + ffn_halo, cols]
        g_buf[0:ffn_halo, cols] = g_buf[tm:tm + ffn_halo, cols]

    chunks = _ffn_chunks(d_ff)

    def ffn_down(ci):
        c0, cw = chunks[ci]
        part = _dot(act_buf.at[ci % 2][:, 0:cw], w_down_ref[c0:c0 + cw, :])
        if ci == 0:
            xo_ref[...] = x1_buf[...] + part
        elif ci < len(chunks) - 1:
            xo_ref[...] += part
        else:
            x2 = xo_ref[...] + part
            if apply_final_norm:
                x2 = x2 * _rms_scale(x2) * fnorm_ref[...]
            xo_ref[...] = x2

    units = _ffn_units(d_ff)
    assert [u[0] for u in units] == [0, 0, 1, 1, 2, 2]
    ffn_dots(units[0])
    mix_project()
    ffn_dots(units[1])
    ffn_dots(units[2])
    ffn_dots(units[3])
    mix_pool_glu()
    ffn_gate(units[0])
    ffn_gate(units[1])
    mix_sgu()
    ffn_dots(units[4])
    ffn_down(0)
    mix_conv(0, tm // 2)
    mix_conv(tm // 2, tm)
    ffn_gate(units[2])
    ffn_gate(units[3])
    ffn_dots(units[5])
    ffn_down(1)
    mix_output()
    ffn_gate(units[4])
    ffn_gate(units[5])
    ffn_down(2)


def _prompt_pool(z_buf, za_buf, mix_buf, pool_tail_ref, s, *, tm, d_a, pool_halo):
    pool_gc = d_a // len(POOL_WINDOWS)
    pool_max = max(POOL_WINDOWS)
    za_buf[pool_halo:pool_halo + tm, :] = z_buf[:, 0:d_a]
    lane_group = lax.broadcasted_iota(jnp.int32, (POOL_BLOCK, d_a), 1) // pool_gc
    lane_w = _lane_window(lane_group)
    row = lax.broadcasted_iota(jnp.int32, (POOL_BLOCK, d_a), 0)
    for r0 in range(0, tm, POOL_BLOCK):
        sums = {1: za_buf[r0:r0 + pool_halo + POOL_BLOCK, :]}
        w = 1
        while w < pool_max:
            sums[2 * w] = sums[w] + pltpu.roll(sums[w], w, axis=0)
            w *= 2
        snapshots = {w: sums[w][pool_halo:] for w in POOL_WINDOWS}
        cnt = jnp.minimum(s * tm + r0 + row + 1, lane_w).astype(F32)
        pooled = _pool_window_select(snapshots, lane_group) / cnt - sums[1][pool_halo:]
        mix_buf[r0:r0 + POOL_BLOCK, 0:d_a] = pooled.astype(BF16)
    pool_tail_ref[...] = za_buf[tm:tm + pool_halo, :]
    za_buf[0:pool_halo, :] = za_buf[tm:tm + pool_halo, :]


_LAYER_WEIGHTS = ['norm1', 'w_in', 'pool_w', 'pool_scale', 'sgu_w', 'sgu_b', 'conv_w', 'conv_b',
                  'cnorm_g', 'cnorm_b', 'w_out', 'norm2', 'w_gate', 'w_val', 'ffn_conv_w',
                  'ffn_conv_b', 'w_down']


_UP_HALF = {'w_gate': 0, 'w_val': 1}


def _layer_operands(wts, names):
    return [wts['w_up'] if n in _UP_HALF else wts[n] for n in names]


def _weight_spec(name, arr, layer_of):
    shape, index = arr.shape[1:], (0,) * (arr.ndim - 1)
    if name in _UP_HALF:
        shape, index = shape[:-1] + (shape[-1] // 2,), index[:-1] + (_UP_HALF[name],)
    return pl.BlockSpec((None,) + shape, lambda *g: (layer_of(*g),) + index,
                        pipeline_mode=pl.Buffered(1))


def _prompt_layer(x, layer, wts, *, apply_final_norm):
    n_b, n_s, d = x.shape
    tm = PROMPT_ROWS
    d_a, d_c, d_ff = wts['pool_w'].shape[-1], wts['conv_w'].shape[-1], wts['w_down'].shape[1]
    d_b = wts['sgu_b'].shape[-1]
    conv_k, ffn_k = wts['conv_w'].shape[1], wts['ffn_conv_w'].shape[1]
    pool_halo = _round_up(max(POOL_WINDOWS) - 1, SUBLANES)
    conv_halo = _round_up(conv_k - 1, SUBLANES)
    ffn_halo = _round_up(ffn_k - 1, SUBLANES)
    assert n_s % tm == 0 and tm % SGU_CHUNK == 0 and (tm // 2) % ROW_BLOCK == 0
    assert tm % POOL_BLOCK == 0 and tm % FFN_ROWS == 0 and d_ff % FFN_LANES == 0
    assert all(w & (w - 1) == 0 for w in POOL_WINDOWS)

    tiles_per_seq = n_s // tm
    n_tiles = n_b * tiles_per_seq

    def mixer_tile(j):
        t = jnp.minimum(j, n_tiles - 1)
        return t // tiles_per_seq, t % tiles_per_seq

    def ffn_tile(j):
        t = jnp.maximum(j - 1, 0)
        return t // tiles_per_seq, t % tiles_per_seq

    operands = _layer_operands(wts, _LAYER_WEIGHTS)
    in_specs = ([pl.BlockSpec((None, tm, d), lambda j: (*mixer_tile(j), 0))]
                + [_weight_spec(n, a, lambda j: layer) for n, a in zip(_LAYER_WEIGHTS, operands)]
                + [pl.BlockSpec(wts['final_norm'].shape, lambda j: (0, 0),
                                pipeline_mode=pl.Buffered(1))])
    out_shape = [jax.ShapeDtypeStruct((n_b, n_s, d), F32),
                 jax.ShapeDtypeStruct((n_b, pool_halo, d_a), F32),
                 jax.ShapeDtypeStruct((n_b, conv_halo, d_c), F32),
                 jax.ShapeDtypeStruct((n_b, ffn_halo, d_ff), F32)]
    out_specs = [pl.BlockSpec((None, tm, d), lambda j: (*ffn_tile(j), 0)),
                 pl.BlockSpec((None, pool_halo, d_a), lambda j: (mixer_tile(j)[0], 0, 0)),
                 pl.BlockSpec((None, conv_halo, d_c), lambda j: (mixer_tile(j)[0], 0, 0)),
                 pl.BlockSpec((None, ffn_halo, d_ff), lambda j: (ffn_tile(j)[0], 0, 0))]
    cw_max = min(FFN_CHUNK, d_ff)
    scratch = [pltpu.VMEM((tm, wts['w_in'].shape[-1]), F32),
               pltpu.VMEM((pool_halo + tm, d_a), F32),
               pltpu.VMEM((conv_halo + tm, d_c), F32),
               pltpu.VMEM((SUBLANES - 1, conv_halo + tm // 2, d_c), F32),
               pltpu.VMEM((conv_k * SUBLANES, d_c), F32),
               pltpu.VMEM((ffn_halo + tm, d_ff), F32),
               pltpu.VMEM((tm, d_ff), F32),
               pltpu.VMEM((2, tm, cw_max), BF16),
               pltpu.VMEM((tm, d), BF16),
               pltpu.VMEM((tm, d), F32),
               pltpu.VMEM((tm, d), BF16)]
    body = functools.partial(
        _prompt_layer_body, tm=tm, n_tiles=n_tiles, tiles_per_seq=tiles_per_seq,
        d_a=d_a, d_b=d_b, d_c=d_c, d_ff=d_ff,
        pool_halo=pool_halo, conv_halo=conv_halo, ffn_halo=ffn_halo,
        conv_k=conv_k, ffn_k=ffn_k, apply_final_norm=apply_final_norm)
    return pl.pallas_call(
        body, grid=(n_tiles + 1,), in_specs=in_specs, out_specs=out_specs,
        out_shape=out_shape, scratch_shapes=scratch, name=f'prompt_layer_{layer}',
        compiler_params=pltpu.CompilerParams(
            dimension_semantics=('arbitrary',), vmem_limit_bytes=VMEM_LIMIT_BYTES),
    )(x, *operands, wts['final_norm'])


def _shift_in_rows(prev_ref, new_ref, n_new):
    n_state = prev_ref.shape[0]
    for j in range(n_state - n_new):
        new_ref[j] = prev_ref[j + n_new]


def _sample_body(
        x_ref, pool_st_ref, conv_st_ref, ffn_st_ref,
        g1_ref, w_in_ref, pool_w_ref, pool_scale_ref, sgu_w_ref, sgu_b_ref,
        conv_w_ref, conv_b_ref, cn_g_ref, cn_b_ref, w_out_ref, g2_ref,
        w_gate_ref, w_val_ref, fconv_w_ref, fconv_b_ref, w_down_ref, fnorm_ref,
        xo_ref, pool_new_ref, conv_new_ref, ffn_new_ref, v_ref,
        x_all, z_buf, g_buf, mix_buf, acc_buf,
        *, depth, n_t, n_seq, d_a, d_b, d_c, d_ff, conv_k, ffn_k):
    layer, blk = pl.program_id(0), pl.program_id(1)
    pool_gc = d_a // len(POOL_WINDOWS)
    pool_max = max(POOL_WINDOWS)
    n_rows = n_t * n_seq
    n_pool, n_conv = pool_max - 1, conv_k - 1

    def slab(t):
        return slice(t * n_seq, (t + 1) * n_seq)

    @pl.when(layer == 0)
    def _load_tokens():
        for t in range(n_t):
            x_all[blk, slab(t), :] = x_ref[t]

    x = x_all[blk]
    h = (x * _rms_scale(x) * g1_ref[...]).astype(BF16)
    z_buf[...] = _dot(h, w_in_ref[...])
    o_u, o_v, o_c, o_g = d_a, d_a + d_b, d_a + 2 * d_b, d_a + 2 * d_b + d_c
    _shift_in_rows(pool_st_ref, pool_new_ref, n_t)
    _shift_in_rows(conv_st_ref, conv_new_ref, n_t)
    for t in range(n_t):
        pool_new_ref[n_pool - n_t + t] = z_buf[slab(t), 0:d_a]
        v_ref[t] = z_buf[slab(t), o_v:o_c]

    lane_group = lax.broadcasted_iota(jnp.int32, (n_seq, d_a), 1) // pool_gc
    lane_w = _lane_window(lane_group)

    def pool_row(j):
        return pool_st_ref[j] if j < n_pool else z_buf[slab(j - n_pool), 0:d_a]

    for t in range(n_t):
        cur = pool_row(n_pool + t)
        acc = cur
        snapshots = {}
        for i in range(1, pool_max):
            acc = acc + pool_row(n_pool + t - i)
            if i + 1 in POOL_WINDOWS:
                snapshots[i + 1] = acc
        cnt = jnp.minimum(PAST_LEN + t + 1, lane_w).astype(F32)
        pooled = _pool_window_select(snapshots, lane_group) / cnt - cur
        mix_buf[slab(t), 0:d_a] = pooled.astype(BF16)
    mixed = _dot(mix_buf[:, 0:d_a], pool_w_ref[...]) * pool_scale_ref[...]
    mix_buf[:, 0:d_a] = mixed.astype(BF16)

    for t in range(n_t):
        gate = jnp.broadcast_to(sgu_b_ref[t:t + 1, :], (n_seq, d_b))
        for sp in range(t + 1):
            gate = gate + sgu_w_ref[t * n_t + sp:t * n_t + sp + 1, :] * z_buf[slab(sp), o_v:o_c]
        mix_buf[slab(t), d_a:d_a + d_b] = (z_buf[slab(t), o_u:o_v] * gate).astype(BF16)

    for t in range(n_t):
        conv_new_ref[n_conv - n_t + t] = (z_buf[slab(t), o_c:o_g]
                                          * jax.nn.sigmoid(z_buf[slab(t), o_g:o_g + d_c]))

    def conv_row(j):
        return conv_st_ref[j] if j < n_conv else conv_new_ref[j - n_t]

    for t in range(n_t):
        acc = jnp.broadcast_to(conv_b_ref[...], (n_seq, d_c))
        for k in range(conv_k):
            acc = acc + conv_w_ref[k:k + 1, :] * conv_row(t + k)
        c_out = _layernorm_silu(acc, cn_g_ref[...], cn_b_ref[...])
        mix_buf[slab(t), d_a + d_b:d_a + d_b + d_c] = c_out.astype(BF16)

    x1 = x + _dot(mix_buf[...], w_out_ref[...])
    acc_buf[...] = x1

    h2 = (x1 * _rms_scale(x1) * g2_ref[...]).astype(BF16)
    n_keep = ffn_k - 1
    for c0, cw in _ffn_chunks(d_ff):
        for k in range(n_keep):
            g_buf[slab(k), 0:cw] = ffn_st_ref[k, :, c0:c0 + cw]
        g_buf[n_keep * n_seq:n_keep * n_seq + n_rows, 0:cw] = _dot(h2, w_gate_ref[:, c0:c0 + cw])
        for k in range(n_keep):
            ffn_new_ref[k, :, c0:c0 + cw] = g_buf[slab(n_t + k), 0:cw]
        val = _dot(h2, w_val_ref[:, c0:c0 + cw])
        gate_c = jnp.broadcast_to(fconv_b_ref[:, c0:c0 + cw], (n_rows, cw))
        for k in range(ffn_k):
            gate_c = gate_c + fconv_w_ref[k:k + 1, c0:c0 + cw] * g_buf[k * n_seq:k * n_seq + n_rows, 0:cw]
        act = (_silu(gate_c) * val).astype(BF16)
        acc_buf[...] += _dot(act, w_down_ref[c0:c0 + cw, :])

    x2 = acc_buf[...]
    x_all[blk] = x2
    y = jnp.where(layer == depth - 1, x2 * _rms_scale(x2) * fnorm_ref[...], x2)
    for t in range(n_t):
        xo_ref[t] = y[slab(t)]


def _sample_trunk(x, pool_st, conv_st, ffn_st, wts):
    n_t, n_all, d = x.shape
    depth = pool_st.shape[0]
    n_seq = min(SAMPLE_SEQS, n_all)
    n_blocks = n_all // n_seq
    d_a, d_c, d_ff = wts['pool_w'].shape[-1], wts['conv_w'].shape[-1], wts['w_down'].shape[1]
    d_b = wts['sgu_b'].shape[-1]
    conv_k, ffn_k = wts['conv_w'].shape[1], wts['ffn_conv_w'].shape[1]
    n_keep = ffn_k - 1
    assert n_all % n_seq == 0 and n_seq % SUBLANES == 0
    assert n_keep <= n_t <= min(conv_k - 1, max(POOL_WINDOWS) - 1)

    def state_spec(n_time, ch):
        return pl.BlockSpec((None, n_time, n_seq, ch), lambda l, i: (l, 0, i, 0))

    names = [n if n not in ('sgu_w', 'sgu_b') else n + '_new' for n in _LAYER_WEIGHTS]
    operands = _layer_operands(wts, names)
    tokens_spec = pl.BlockSpec((n_t, n_seq, d), lambda l, i: (0, i, 0))
    in_specs = ([tokens_spec, state_spec(pool_st.shape[1], d_a), state_spec(conv_st.shape[1], d_c),
                 state_spec(ffn_st.shape[1], d_ff)]
                + [_weight_spec(n, a, lambda l, i: l) for n, a in zip(names, operands)]
                + [pl.BlockSpec(wts['final_norm'].shape, lambda l, i: (0, 0),
                                pipeline_mode=pl.Buffered(1))])
    state_dims = [(pool_st.shape[1], d_a), (conv_st.shape[1], d_c), (n_keep, d_ff), (n_t, d_b)]
    out_shape = ([jax.ShapeDtypeStruct((n_t, n_all, d), F32)]
                 + [jax.ShapeDtypeStruct((depth, nt, n_all, ch), F32) for nt, ch in state_dims])
    out_specs = [tokens_spec] + [state_spec(nt, ch) for nt, ch in state_dims]
    n_rows = n_t * n_seq
    cw_max = min(FFN_CHUNK, d_ff)
    scratch = [pltpu.VMEM((n_blocks, n_rows, d), F32),
               pltpu.VMEM((n_rows, wts['w_in'].shape[-1]), F32),
               pltpu.VMEM((n_keep * n_seq + n_rows, cw_max), F32),
               pltpu.VMEM((n_rows, d), BF16),
               pltpu.VMEM((n_rows, d), F32)]
    body = functools.partial(
        _sample_body, depth=depth, n_t=n_t, n_seq=n_seq, d_a=d_a, d_b=d_b, d_c=d_c, d_ff=d_ff,
        conv_k=conv_k, ffn_k=ffn_k)
    return pl.pallas_call(
        body, grid=(depth, n_blocks), in_specs=in_specs, out_specs=out_specs,
        out_shape=out_shape, scratch_shapes=scratch, name='sample_trunk',
        compiler_params=pltpu.CompilerParams(
            dimension_semantics=('arbitrary', 'arbitrary'), vmem_limit_bytes=VMEM_LIMIT_BYTES),
    )(x, pool_st, conv_st, ffn_st, *operands, wts['final_norm'])


def _block_diag(w):
    depth, n_g, c, _ = w.shape
    eye = jnp.eye(n_g, dtype=w.dtype)
    return jnp.einsum('lgcd,gh->lgchd', w, eye).reshape(depth, n_g * c, n_g * c)


def kernel(x_prompt, x_sample, state_pool, state_conv, state_ffn_conv, norm1, w_in, pool_w, pool_scale, sgu_w, sgu_b, conv_w, conv_b, cnorm_g, cnorm_b, w_out, norm2, w_up, ffn_conv_w, ffn_conv_b, w_down, final_norm):
    depth = norm1.shape[0]
    d_ff = w_down.shape[1]
    d_b = (w_in.shape[-1] - pool_scale.shape[-1] - 2 * conv_w.shape[-1]) // 2
    hd = d_b // SGU_HEADS
    n_t = x_sample.shape[1]

    row = lambda a: a[:, None, :]
    wts = dict(
        norm1=row(norm1), w_in=w_in.astype(BF16),
        pool_w=_block_diag(pool_w).astype(BF16), pool_scale=row(pool_scale),
        sgu_w=sgu_w.reshape(depth, SGU_HEADS * SGU_CHUNK, SGU_CHUNK),
        sgu_b=jnp.repeat(jnp.swapaxes(sgu_b, 1, 2), hd, axis=-1),
        conv_w=conv_w, conv_b=row(conv_b), cnorm_g=row(cnorm_g), cnorm_b=row(cnorm_b),
        w_out=w_out.astype(BF16), norm2=row(norm2),
        w_up=w_up.astype(BF16),
        ffn_conv_w=ffn_conv_w, ffn_conv_b=row(ffn_conv_b), w_down=w_down.astype(BF16),
        final_norm=final_norm[None, :])
    w_new = jnp.transpose(sgu_w[:, :, :n_t, :n_t], (0, 2, 3, 1)).reshape(depth, n_t * n_t, SGU_HEADS)
    wts['sgu_w_new'] = jnp.repeat(w_new, hd, axis=-1)
    wts['sgu_b_new'] = jnp.repeat(jnp.swapaxes(sgu_b[:, :, :n_t], 1, 2), hd, axis=-1)

    x = x_prompt
    pool_p, conv_p, ffn_p = [], [], []
    for l in range(depth):
        x, pt, ct, ft = _prompt_layer(x, l, wts, apply_final_norm=(l == depth - 1))
        pool_p.append(pt[:, pt.shape[1] - (max(POOL_WINDOWS) - 1):])
        conv_p.append(ct[:, ct.shape[1] - (conv_w.shape[1] - 1):])
        ffn_p.append(ft[:, ft.shape[1] - (ffn_conv_w.shape[1] - 1):])
    y_prompt = x

    swap = lambda a: jnp.swapaxes(a, -3, -2)
    ys, pool_s, conv_s, ffn_s, v_s = _sample_trunk(
        swap(x_sample), swap(state_pool), swap(state_conv), swap(state_ffn_conv), wts)

    return (y_prompt, swap(ys),
            jnp.stack(pool_p), swap(pool_s),
            jnp.stack(conv_p), swap(conv_s),
            jnp.stack(ffn_p), swap(ffn_s),
            swap(v_s))
```

```python
import functools

import jax
import jax.numpy as jnp
from jax import lax
from jax.experimental import pallas as pl
from jax.experimental.pallas import tpu as pltpu

POOL_WINDOWS = (2, 4, 8, 16)
SGU_HEADS = 4
SGU_CHUNK = 128
PAST_LEN = 16384
EPS = 1e-6

LANES = 128
SUBLANES = 8
MXU_DIM = 256
VMEM_LIMIT_BYTES = 60 * 1024 * 1024

PROMPT_ROWS = 512
SAMPLE_SEQS = 64
ROW_BLOCK = 32
POOL_BLOCK = 128
FFN_CHUNK = 4 * MXU_DIM
FFN_UNIT = 2 * MXU_DIM
FFN_ROWS = 128
FFN_LANES = 2 * LANES

F32 = jnp.float32
BF16 = jnp.bfloat16


def _round_up(n, m):
    return -(-n // m) * m


def _rms_scale(x):
    return lax.rsqrt(jnp.mean(x * x, axis=-1, keepdims=True) + EPS)


def _silu(x):
    return x * jax.nn.sigmoid(x)


def _layernorm_silu(y, g, b):
    mu = jnp.mean(y, axis=-1, keepdims=True)
    d = y - mu
    var = jnp.mean(d * d, axis=-1, keepdims=True)
    return _silu(d * lax.rsqrt(var + EPS) * g + b)


def _dot(a, b):
    return jnp.dot(a, b, preferred_element_type=F32)


def _ffn_chunks(d_ff):
    return [(c0, min(FFN_CHUNK, d_ff - c0)) for c0 in range(0, d_ff, FFN_CHUNK)]


def _pool_window_select(snapshots, lane_group):
    out = snapshots[POOL_WINDOWS[0]]
    for g, w in enumerate(POOL_WINDOWS[1:], start=1):
        out = jnp.where(lane_group >= g, snapshots[w], out)
    return out


def _lane_window(lane_group):
    w = jnp.full(lane_group.shape, POOL_WINDOWS[0], jnp.int32)
    for g, win in enumerate(POOL_WINDOWS[1:], start=1):
        w = jnp.where(lane_group >= g, win, w)
    return w


def _ffn_units(d_ff):
    return [(ci, c0, cw, u0, min(FFN_UNIT, cw - u0))
            for ci, (c0, cw) in enumerate(_ffn_chunks(d_ff)) for u0 in range(0, cw, FFN_UNIT)]


def _prompt_layer_body(
        x_ref, g1_ref, w_in_ref, pool_w_ref, pool_scale_ref, sgu_w_ref, sgu_b_ref,
        conv_w_ref, conv_b_ref, cn_g_ref, cn_b_ref, w_out_ref, g2_ref,
        w_gate_ref, w_val_ref, fconv_w_ref, fconv_b_ref, w_down_ref, fnorm_ref,
        xo_ref, pool_tail_ref, conv_tail_ref, ffn_tail_ref,
        z_buf, za_buf, c_buf, rot_buf, wb_buf, g_buf, val_buf, act_buf, mix_buf, x1_buf, h2_buf,
        *, tm, n_tiles, tiles_per_seq, d_a, d_b, d_c, d_ff, pool_halo, conv_halo, ffn_halo,
        conv_k, ffn_k, apply_final_norm):
    j = pl.program_id(0)
    s = lax.rem(jnp.minimum(j, n_tiles - 1), tiles_per_seq)
    o_u, o_v, o_c, o_g = d_a, d_a + d_b, d_a + 2 * d_b, d_a + 2 * d_b + d_c

    @pl.when(j == 0)
    def _no_ffn_tile_yet():
        x1_buf[...] = jnp.zeros(x1_buf.shape, F32)
        h2_buf[...] = jnp.zeros(h2_buf.shape, BF16)
        for k in range(conv_k):
            wb_buf[k * SUBLANES:(k + 1) * SUBLANES, :] = jnp.broadcast_to(conv_w_ref[k:k + 1, :], (SUBLANES, d_c))

    @pl.when(s == 0)
    def _zero_mixer_left_state():
        za_buf[0:pool_halo, :] = jnp.zeros((pool_halo, d_a), F32)
        c_buf[0:conv_halo, :] = jnp.zeros((conv_halo, d_c), F32)

    @pl.when((j == 0) | (lax.rem(j + tiles_per_seq - 1, tiles_per_seq) == 0))
    def _zero_ffn_left_state():
        g_buf[0:ffn_halo, :] = jnp.zeros((ffn_halo, d_ff), F32)

    def mix_project():
        x = x_ref[...]
        h = (x * _rms_scale(x) * g1_ref[...]).astype(BF16)
        z_buf[...] = _dot(h, w_in_ref[...])

    def mix_pool_glu():
        _prompt_pool(z_buf, za_buf, mix_buf, pool_tail_ref, s, tm=tm, d_a=d_a, pool_halo=pool_halo)
        c_buf[conv_halo:conv_halo + tm, :] = z_buf[:, o_c:o_g] * jax.nn.sigmoid(z_buf[:, o_g:o_g + d_c])

    def mix_sgu():
        n_hd = SGU_HEADS * SGU_CHUNK
        w_row = lax.broadcasted_iota(jnp.int32, (n_hd, SGU_CHUNK), 0) % SGU_CHUNK
        w_col = lax.broadcasted_iota(jnp.int32, (n_hd, SGU_CHUNK), 1)
        w_tril = jnp.where(w_col <= w_row, sgu_w_ref[...], 0.0).astype(BF16)
        head_of_lane = lax.broadcasted_iota(jnp.int32, (SGU_CHUNK, d_b), 1) // (d_b // SGU_HEADS)
        for r0 in range(0, tm, SGU_CHUNK):
            v = z_buf[r0:r0 + SGU_CHUNK, o_v:o_c].astype(BF16)
            per_head = _dot(w_tril, v)
            gate = per_head[0:SGU_CHUNK]
            for hd in range(1, SGU_HEADS):
                gate = jnp.where(head_of_lane >= hd,
                                 per_head[hd * SGU_CHUNK:(hd + 1) * SGU_CHUNK], gate)
            b_out = z_buf[r0:r0 + SGU_CHUNK, o_u:o_v] * (gate + sgu_b_ref[...])
            mix_buf[r0:r0 + SGU_CHUNK, d_a:d_a + d_b] = b_out.astype(BF16)

    def mix_conv(row_lo, row_hi):
        window = c_buf[row_lo:row_hi + conv_halo, :]
        for r in range(1, SUBLANES):
            rot_buf[r - 1] = pltpu.roll(window, r, axis=0)
        for r0 in range(0, row_hi - row_lo, ROW_BLOCK):
            acc = jnp.broadcast_to(conv_b_ref[...], (ROW_BLOCK, d_c))
            for k in range(conv_k):
                off = conv_halo - (conv_k - 1) + k
                r = -off % SUBLANES
                lo = off + r + r0
                tap = (c_buf[row_lo + lo:row_lo + lo + ROW_BLOCK, :] if r == 0
                       else rot_buf[r - 1, lo:lo + ROW_BLOCK, :])
                w_k = wb_buf[k * SUBLANES:(k + 1) * SUBLANES, :]
                acc = acc + jnp.concatenate([w_k] * (ROW_BLOCK // SUBLANES), axis=0) * tap
            c_out = _layernorm_silu(acc, cn_g_ref[...], cn_b_ref[...])
            mix_buf[row_lo + r0:row_lo + r0 + ROW_BLOCK, d_a + d_b:d_a + d_b + d_c] = c_out.astype(BF16)

    def mix_output():
        conv_tail_ref[...] = c_buf[tm:tm + conv_halo, :]
        c_buf[0:conv_halo, :] = c_buf[tm:tm + conv_halo, :]
        mixed = _dot(mix_buf[:, 0:d_a], pool_w_ref[...]) * pool_scale_ref[...]
        mix_buf[:, 0:d_a] = mixed.astype(BF16)
        x1 = x_ref[...] + _dot(mix_buf[...], w_out_ref[...])
        x1_buf[...] = x1
        h2_buf[...] = (x1 * _rms_scale(x1) * g2_ref[...]).astype(BF16)

    def ffn_dots(unit):
        _, c0, _, u0, uw = unit
        cols = slice(c0 + u0, c0 + u0 + uw)
        g_buf[ffn_halo:ffn_halo + tm, cols] = _dot(h2_buf[...], w_gate_ref[:, cols])
        val_buf[:, cols] = _dot(h2_buf[...], w_val_ref[:, cols])

    def ffn_gate(unit):
        ci, c0, _, u0, uw = unit
        act_cur = act_buf.at[ci % 2]
        for r0 in range(0, tm, FFN_ROWS):
            for l0 in range(0, uw, FFN_LANES):
                cols = slice(c0 + u0 + l0, c0 + u0 + l0 + FFN_LANES)
                g_win = g_buf[r0:r0 + ffn_halo + FFN_ROWS, cols]
                gate_c = jnp.broadcast_to(fconv_b_ref[:, cols], (FFN_ROWS, FFN_LANES))
                for k in range(ffn_k):
                    shift = ffn_k - 1 - k
                    tap = g_win if shift == 0 else pltpu.roll(g_win, shift, axis=0)
                    gate_c = gate_c + fconv_w_ref[k:k + 1, cols] * tap[ffn_halo:]
                act = _silu(gate_c) * val_buf[r0:r0 + FFN_ROWS, cols]
                act_cur[r0:r0 + FFN_ROWS, u0 + l0:u0 + l0 + FFN_LANES] = act.astype(BF16)
        cols = slice(c0 + u0, c0 + u0 + uw)
        ffn_tail_ref[:, cols] = g_buf[tm:tm + ffn_halo, cols]
        g_buf[0:ffn_halo, cols] = g_buf[tm:tm + ffn_halo, cols]

    chunks = _ffn_chunks(d_ff)

    def ffn_down(ci):
        c0, cw = chunks[ci]
        part = _dot(act_buf.at[ci % 2][:, 0:cw], w_down_ref[c0:c0 + cw, :])
        if ci == 0:
            xo_ref[...] = x1_buf[...] + part
        elif ci < len(chunks) - 1:
            xo_ref[...] += part
        else:
            x2 = xo_ref[...] + part
            if apply_final_norm:
                x2 = x2 * _rms_scale(x2) * fnorm_ref[...]
            xo_ref[...] = x2

    units = _ffn_units(d_ff)
    assert [u[0] for u in units] == [0, 0, 1, 1, 2, 2]
    ffn_dots(units[0])
    mix_project()
    ffn_dots(units[1])
    ffn_dots(units[2])
    ffn_dots(units[3])
    mix_pool_glu()
    ffn_gate(units[0])
    ffn_gate(units[1])
    mix_sgu()
    ffn_dots(units[4])
    ffn_down(0)
    mix_conv(0, tm // 2)
    mix_conv(tm // 2, tm)
    ffn_gate(units[2])
    ffn_gate(units[3])
    ffn_dots(units[5])
    ffn_down(1)
    mix_output()
    ffn_gate(units[4])
    ffn_gate(units[5])
    ffn_down(2)


def _prompt_pool(z_buf, za_buf, mix_buf, pool_tail_ref, s, *, tm, d_a, pool_halo):
    pool_gc = d_a // len(POOL_WINDOWS)
    pool_max = max(POOL_WINDOWS)
    za_buf[pool_halo:pool_halo + tm, :] = z_buf[:, 0:d_a]
    lane_group = lax.broadcasted_iota(jnp.int32, (POOL_BLOCK, d_a), 1) // pool_gc
    lane_w = _lane_window(lane_group)
    row = lax.broadcasted_iota(jnp.int32, (POOL_BLOCK, d_a), 0)
    for r0 in range(0, tm, POOL_BLOCK):
        sums = {1: za_buf[r0:r0 + pool_halo + POOL_BLOCK, :]}
        w = 1
        while w < pool_max:
            sums[2 * w] = sums[w] + pltpu.roll(sums[w], w, axis=0)
            w *= 2
        snapshots = {w: sums[w][pool_halo:] for w in POOL_WINDOWS}
        cnt = jnp.minimum(s * tm + r0 + row + 1, lane_w).astype(F32)
        pooled = _pool_window_select(snapshots, lane_group) / cnt - sums[1][pool_halo:]
        mix_buf[r0:r0 + POOL_BLOCK, 0:d_a] = pooled.astype(BF16)
    pool_tail_ref[...] = za_buf[tm:tm + pool_halo, :]
    za_buf[0:pool_halo, :] = za_buf[tm:tm + pool_halo, :]


_LAYER_WEIGHTS = ['norm1', 'w_in', 'pool_w', 'pool_scale', 'sgu_w', 'sgu_b', 'conv_w', 'conv_b',
                  'cnorm_g', 'cnorm_b', 'w_out', 'norm2', 'w_gate', 'w_val', 'ffn_conv_w',
                  'ffn_conv_b', 'w_down']


_UP_HALF = {'w_gate': 0, 'w_val': 1}


def _layer_operands(wts, names):
    return [wts['w_up'] if n in _UP_HALF else wts[n] for n in names]


def _weight_spec(name, arr, layer_of):
    shape, index = arr.shape[1:], (0,) * (arr.ndim - 1)
    if name in _UP_HALF:
        shape, index = shape[:-1] + (shape[-1] // 2,), index[:-1] + (_UP_HALF[name],)
    return pl.BlockSpec((None,) + shape, lambda *g: (layer_of(*g),) + index,
                        pipeline_mode=pl.Buffered(1))


def _prompt_layer(x, layer, wts, *, apply_final_norm):
    n_b, n_s, d = x.shape
    tm = PROMPT_ROWS
    d_a, d_c, d_ff = wts['pool_w'].shape[-1], wts['conv_w'].shape[-1], wts['w_down'].shape[1]
    d_b = wts['sgu_b'].shape[-1]
    conv_k, ffn_k = wts['conv_w'].shape[1], wts['ffn_conv_w'].shape[1]
    pool_halo = _round_up(max(POOL_WINDOWS) - 1, SUBLANES)
    conv_halo = _round_up(conv_k - 1, SUBLANES)
    ffn_halo = _round_up(ffn_k - 1, SUBLANES)
    assert n_s % tm == 0 and tm % SGU_CHUNK == 0 and (tm // 2) % ROW_BLOCK == 0
    assert tm % POOL_BLOCK == 0 and tm % FFN_ROWS == 0 and d_ff % FFN_LANES == 0
    assert all(w & (w - 1) == 0 for w in POOL_WINDOWS)

    tiles_per_seq = n_s // tm
    n_tiles = n_b * tiles_per_seq

    def mixer_tile(j):
        t = jnp.minimum(j, n_tiles - 1)
        return t // tiles_per_seq, t % tiles_per_seq

    def ffn_tile(j):
        t = jnp.maximum(j - 1, 0)
        return t // tiles_per_seq, t % tiles_per_seq

    operands = _layer_operands(wts, _LAYER_WEIGHTS)
    in_specs = ([pl.BlockSpec((None, tm, d), lambda j: (*mixer_tile(j), 0))]
                + [_weight_spec(n, a, lambda j: layer) for n, a in zip(_LAYER_WEIGHTS, operands)]
                + [pl.BlockSpec(wts['final_norm'].shape, lambda j: (0, 0),
                                pipeline_mode=pl.Buffered(1))])
    out_shape = [jax.ShapeDtypeStruct((n_b, n_s, d), F32),
                 jax.ShapeDtypeStruct((n_b, pool_halo, d_a), F32),
                 jax.ShapeDtypeStruct((n_b, conv_halo, d_c), F32),
                 jax.ShapeDtypeStruct((n_b, ffn_halo, d_ff), F32)]
    out_specs = [pl.BlockSpec((None, tm, d), lambda j: (*ffn_tile(j), 0)),
                 pl.BlockSpec((None, pool_halo, d_a), lambda j: (mixer_tile(j)[0], 0, 0)),
                 pl.BlockSpec((None, conv_halo, d_c), lambda j: (mixer_tile(j)[0], 0, 0)),
                 pl.BlockSpec((None, ffn_halo, d_ff), lambda j: (ffn_tile(j)[0], 0, 0))]
    cw_max = min(FFN_CHUNK, d_ff)
    scratch = [pltpu.VMEM((tm, wts['w_in'].shape[-1]), F32),
               pltpu.VMEM((pool_halo + tm, d_a), F32),
               pltpu.VMEM((conv_halo + tm, d_c), F32),
               pltpu.VMEM((SUBLANES - 1, conv_halo + tm // 2, d_c), F32),
               pltpu.VMEM((conv_k * SUBLANES, d_c), F32),
               pltpu.VMEM((ffn_halo + tm, d_ff), F32),
               pltpu.VMEM((tm, d_ff), F32),
               pltpu.VMEM((2, tm, cw_max), BF16),
               pltpu.VMEM((tm, d), BF16),
               pltpu.VMEM((tm, d), F32),
               pltpu.VMEM((tm, d), BF16)]
    body = functools.partial(
        _prompt_layer_body, tm=tm, n_tiles=n_tiles, tiles_per_seq=tiles_per_seq,
        d_a=d_a, d_b=d_b, d_c=d_c, d_ff=d_ff,
        pool_halo=pool_halo, conv_halo=conv_halo, ffn_halo=ffn_halo,
        conv_k=conv_k, ffn_k=ffn_k, apply_final_norm=apply_final_norm)
    return pl.pallas_call(
        body, grid=(n_tiles + 1,), in_specs=in_specs, out_specs=out_specs,
        out_shape=out_shape, scratch_shapes=scratch, name=f'prompt_layer_{layer}',
        compiler_params=pltpu.CompilerParams(
            dimension_semantics=('arbitrary',), vmem_limit_bytes=VMEM_LIMIT_BYTES),
    )(x, *operands, wts['final_norm'])


def _shift_in_rows(prev_ref, new_ref, n_new):
    n_state = prev_ref.shape[0]
    for j in range(n_state - n_new):
        new_ref[j] = prev_ref[j + n_new]


def _sample_body(
        x_ref, pool_st_ref, conv_st_ref, ffn_st_ref,
        g1_ref, w_in_ref, pool_w_ref, pool_scale_ref, sgu_w_ref, sgu_b_ref,
        conv_w_ref, conv_b_ref, cn_g_ref, cn_b_ref, w_out_ref, g2_ref,
        w_gate_ref, w_val_ref, fconv_w_ref, fconv_b_ref, w_down_ref, fnorm_ref,
        xo_ref, pool_new_ref, conv_new_ref, ffn_new_ref, v_ref,
        x_all, z_buf, g_buf, mix_buf, acc_buf,
        *, depth, n_t, n_seq, d_a, d_b, d_c, d_ff, conv_k, ffn_k):
    layer, blk = pl.program_id(0), pl.program_id(1)
    pool_gc = d_a // len(POOL_WINDOWS)
    pool_max = max(POOL_WINDOWS)
    n_rows = n_t * n_seq
    n_pool, n_conv = pool_max - 1, conv_k - 1

    def slab(t):
        return slice(t * n_seq, (t + 1) * n_seq)

    @pl.when(layer == 0)
    def _load_tokens():
        for t in range(n_t):
            x_all[blk, slab(t), :] = x_ref[t]

    x = x_all[blk]
    h = (x * _rms_scale(x) * g1_ref[...]).astype(BF16)
    z_buf[...] = _dot(h, w_in_ref[...])
    o_u, o_v, o_c, o_g = d_a, d_a + d_b, d_a + 2 * d_b, d_a + 2 * d_b + d_c
    _shift_in_rows(pool_st_ref, pool_new_ref, n_t)
    _shift_in_rows(conv_st_ref, conv_new_ref, n_t)
    for t in range(n_t):
        pool_new_ref[n_pool - n_t + t] = z_buf[slab(t), 0:d_a]
        v_ref[t] = z_buf[slab(t), o_v:o_c]

    lane_group = lax.broadcasted_iota(jnp.int32, (n_seq, d_a), 1) // pool_gc
    lane_w = _lane_window(lane_group)

    def pool_row(j):
        return pool_st_ref[j] if j < n_pool else z_buf[slab(j - n_pool), 0:d_a]

    for t in range(n_t):
        cur = pool_row(n_pool + t)
        acc = cur
        snapshots = {}
        for i in range(1, pool_max):
            acc = acc + pool_row(n_pool + t - i)
            if i + 1 in POOL_WINDOWS:
                snapshots[i + 1] = acc
        cnt = jnp.minimum(PAST_LEN + t + 1, lane_w).astype(F32)
        pooled = _pool_window_select(snapshots, lane_group) / cnt - cur
        mix_buf[slab(t), 0:d_a] = pooled.astype(BF16)
    mixed = _dot(mix_buf[:, 0:d_a], pool_w_ref[...]) * pool_scale_ref[...]
    mix_buf[:, 0:d_a] = mixed.astype(BF16)

    for t in range(n_t):
        gate = jnp.broadcast_to(sgu_b_ref[t:t + 1, :], (n_seq, d_b))
        for sp in range(t + 1):
            gate = gate + sgu_w_ref[t * n_t + sp:t * n_t + sp + 1, :] * z_buf[slab(sp), o_v:o_c]
        mix_buf[slab(t), d_a:d_a + d_b] = (z_buf[slab(t), o_u:o_v] * gate).astype(BF16)

    for t in range(n_t):
        conv_new_ref[n_conv - n_t + t] = (z_buf[slab(t), o_c:o_g]
                                          * jax.nn.sigmoid(z_buf[slab(t), o_g:o_g + d_c]))

    def conv_row(j):
        return conv_st_ref[j] if j < n_conv else conv_new_ref[j - n_t]

    for t in range(n_t):
        acc = jnp.broadcast_to(conv_b_ref[...], (n_seq, d_c))
        for k in range(conv_k):
            acc = acc + conv_w_ref[k:k + 1, :] * conv_row(t + k)
        c_out = _layernorm_silu(acc, cn_g_ref[...], cn_b_ref[...])
        mix_buf[slab(t), d_a + d_b:d_a + d_b + d_c] = c_out.astype(BF16)

    x1 = x + _dot(mix_buf[...], w_out_ref[...])
    acc_buf[...] = x1

    h2 = (x1 * _rms_scale(x1) * g2_ref[...]).astype(BF16)
    n_keep = ffn_k - 1
    for c0, cw in _ffn_chunks(d_ff):
        for k in range(n_keep):
            g_buf[slab(k), 0:cw] = ffn_st_ref[k, :, c0:c0 + cw]
        g_buf[n_keep * n_seq:n_keep * n_seq + n_rows, 0:cw] = _dot(h2, w_gate_ref[:, c0:c0 + cw])
        for k in range(n_keep):
            ffn_new_ref[k, :, c0:c0 + cw] = g_buf[slab(n_t + k), 0:cw]
        val = _dot(h2, w_val_ref[:, c0:c0 + cw])
        gate_c = jnp.broadcast_to(fconv_b_ref[:, c0:c0 + cw], (n_rows, cw))
        for k in range(ffn_k):
            gate_c = gate_c + fconv_w_ref[k:k + 1, c0:c0 + cw] * g_buf[k * n_seq:k * n_seq + n_rows, 0:cw]
        act = (_silu(gate_c) * val).astype(BF16)
        acc_buf[...] += _dot(act, w_down_ref[c0:c0 + cw, :])

    x2 = acc_buf[...]
    x_all[blk] = x2

    @pl.when(layer == depth - 1)
    def _emit_tokens():
        y = x2 * _rms_scale(x2) * fnorm_ref[...]
        for t in range(n_t):
            xo_ref[t] = y[slab(t)]


def _sample_trunk(x, pool_st, conv_st, ffn_st, wts):
    n_t, n_all, d = x.shape
    depth = pool_st.shape[0]
    n_seq = min(SAMPLE_SEQS, n_all)
    n_blocks = n_all // n_seq
    d_a, d_c, d_ff = wts['pool_w'].shape[-1], wts['conv_w'].shape[-1], wts['w_down'].shape[1]
    d_b = wts['sgu_b'].shape[-1]
    conv_k, ffn_k = wts['conv_w'].shape[1], wts['ffn_conv_w'].shape[1]
    n_keep = ffn_k - 1
    assert n_all % n_seq == 0 and n_seq % SUBLANES == 0
    assert n_keep <= n_t <= min(conv_k - 1, max(POOL_WINDOWS) - 1)

    def state_spec(n_time, ch):
        return pl.BlockSpec((None, n_time, n_seq, ch), lambda l, i: (l, 0, i, 0))

    names = [n if n not in ('sgu_w', 'sgu_b') else n + '_new' for n in _LAYER_WEIGHTS]
    operands = _layer_operands(wts, names)
    tokens_spec = pl.BlockSpec((n_t, n_seq, d), lambda l, i: (0, i, 0))
    in_specs = ([tokens_spec, state_spec(pool_st.shape[1], d_a), state_spec(conv_st.shape[1], d_c),
                 state_spec(ffn_st.shape[1], d_ff)]
                + [_weight_spec(n, a, lambda l, i: l) for n, a in zip(names, operands)]
                + [pl.BlockSpec(wts['final_norm'].shape, lambda l, i: (0, 0),
                                pipeline_mode=pl.Buffered(1))])
    state_dims = [(pool_st.shape[1], d_a), (conv_st.shape[1], d_c), (n_keep, d_ff), (n_t, d_b)]
    out_shape = ([jax.ShapeDtypeStruct((n_t, n_all, d), F32)]
                 + [jax.ShapeDtypeStruct((depth, nt, n_all, ch), F32) for nt, ch in state_dims])
    y_spec = pl.BlockSpec((n_t, n_seq, d), lambda l, i: (0, jnp.where(l == depth - 1, i, 0), 0))
    out_specs = [y_spec] + [state_spec(nt, ch) for nt, ch in state_dims]
    n_rows = n_t * n_seq
    cw_max = min(FFN_CHUNK, d_ff)
    scratch = [pltpu.VMEM((n_blocks, n_rows, d), F32),
               pltpu.VMEM((n_rows, wts['w_in'].shape[-1]), F32),
               pltpu.VMEM((n_keep * n_seq + n_rows, cw_max), F32),
               pltpu.VMEM((n_rows, d), BF16),
               pltpu.VMEM((n_rows, d), F32)]
    body = functools.partial(
        _sample_body, depth=depth, n_t=n_t, n_seq=n_seq, d_a=d_a, d_b=d_b, d_c=d_c, d_ff=d_ff,
        conv_k=conv_k, ffn_k=ffn_k)
    return pl.pallas_call(
        body, grid=(depth, n_blocks), in_specs=in_specs, out_specs=out_specs,
        out_shape=out_shape, scratch_shapes=scratch, name='sample_trunk',
        compiler_params=pltpu.CompilerParams(
            dimension_semantics=('arbitrary', 'arbitrary'), vmem_limit_bytes=VMEM_LIMIT_BYTES),
    )(x, pool_st, conv_st, ffn_st, *operands, wts['final_norm'])


def _block_diag(w):
    depth, n_g, c, _ = w.shape
    eye = jnp.eye(n_g, dtype=w.dtype)
    return jnp.einsum('lgcd,gh->lgchd', w, eye).reshape(depth, n_g * c, n_g * c)


def kernel(x_prompt, x_sample, state_pool, state_conv, state_ffn_conv, norm1, w_in, pool_w, pool_scale, sgu_w, sgu_b, conv_w, conv_b, cnorm_g, cnorm_b, w_out, norm2, w_up, ffn_conv_w, ffn_conv_b, w_down, final_norm):
    depth = norm1.shape[0]
    d_ff = w_down.shape[1]
    d_b = (w_in.shape[-1] - pool_scale.shape[-1] - 2 * conv_w.shape[-1]) // 2
    hd = d_b // SGU_HEADS
    n_t = x_sample.shape[1]

    row = lambda a: a[:, None, :]
    wts = dict(
        norm1=row(norm1), w_in=w_in.astype(BF16),
        pool_w=_block_diag(pool_w).astype(BF16), pool_scale=row(pool_scale),
        sgu_w=sgu_w.reshape(depth, SGU_HEADS * SGU_CHUNK, SGU_CHUNK),
        sgu_b=jnp.repeat(jnp.swapaxes(sgu_b, 1, 2), hd, axis=-1),
        conv_w=conv_w, conv_b=row(conv_b), cnorm_g=row(cnorm_g), cnorm_b=row(cnorm_b),
        w_out=w_out.astype(BF16), norm2=row(norm2),
        w_up=w_up.astype(BF16),
        ffn_conv_w=ffn_conv_w, ffn_conv_b=row(ffn_conv_b), w_down=w_down.astype(BF16),
        final_norm=final_norm[None, :])
    w_new = jnp.transpose(sgu_w[:, :, :n_t, :n_t], (0, 2, 3, 1)).reshape(depth, n_t * n_t, SGU_HEADS)
    wts['sgu_w_new'] = jnp.repeat(w_new, hd, axis=-1)
    wts['sgu_b_new'] = jnp.repeat(jnp.swapaxes(sgu_b[:, :, :n_t], 1, 2), hd, axis=-1)

    x = x_prompt
    pool_p, conv_p, ffn_p = [], [], []
    for l in range(depth):
        x, pt, ct, ft = _prompt_layer(x, l, wts, apply_final_norm=(l == depth - 1))
        pool_p.append(pt[:, pt.shape[1] - (max(POOL_WINDOWS) - 1):])
        conv_p.append(ct[:, ct.shape[1] - (conv_w.shape[1] - 1):])
        ffn_p.append(ft[:, ft.shape[1] - (ffn_conv_w.shape[1] - 1):])
    y_prompt = x

    swap = lambda a: jnp.swapaxes(a, -3, -2)
    ys, pool_s, conv_s, ffn_s, v_s = _sample_trunk(
        swap(x_sample), swap(state_pool), swap(state_conv), swap(state_ffn_conv), wts)

    return (y_prompt, swap(ys),
            jnp.stack(pool_p), swap(pool_s),
            jnp.stack(conv_p), swap(conv_s),
            jnp.stack(ffn_p), swap(ffn_s),
            swap(v_s))
```

```python
import functools

import jax
import jax.numpy as jnp
from jax import lax
from jax.experimental import pallas as pl
from jax.experimental.pallas import tpu as pltpu

POOL_WINDOWS = (2, 4, 8, 16)
SGU_HEADS = 4
SGU_CHUNK = 128
PAST_LEN = 16384
EPS = 1e-6

LANES = 128
SUBLANES = 8
MXU_DIM = 256
VMEM_LIMIT_BYTES = 60 * 1024 * 1024

PROMPT_ROWS = 512
SAMPLE_SEQS = 64
ROW_BLOCK = 32
POOL_BLOCK = 128
FFN_CHUNK = 4 * MXU_DIM
FFN_UNIT = 2 * MXU_DIM
FFN_ROWS = 128
FFN_LANES = 2 * LANES

F32 = jnp.float32
BF16 = jnp.bfloat16


def _round_up(n, m):
    return -(-n // m) * m


def _rms_scale(x):
    return lax.rsqrt(jnp.mean(x * x, axis=-1, keepdims=True) + EPS)


def _silu(x):
    return x * jax.nn.sigmoid(x)


def _layernorm_silu(y, g, b):
    mu = jnp.mean(y, axis=-1, keepdims=True)
    d = y - mu
    var = jnp.mean(d * d, axis=-1, keepdims=True)
    return _silu(d * lax.rsqrt(var + EPS) * g + b)


def _dot(a, b):
    return jnp.dot(a, b, preferred_element_type=F32)


def _ffn_chunks(d_ff):
    return [(c0, min(FFN_CHUNK, d_ff - c0)) for c0 in range(0, d_ff, FFN_CHUNK)]


def _pool_window_select(snapshots, lane_group):
    out = snapshots[POOL_WINDOWS[0]]
    for g, w in enumerate(POOL_WINDOWS[1:], start=1):
        out = jnp.where(lane_group >= g, snapshots[w], out)
    return out


def _lane_window(lane_group):
    w = jnp.full(lane_group.shape, POOL_WINDOWS[0], jnp.int32)
    for g, win in enumerate(POOL_WINDOWS[1:], start=1):
        w = jnp.where(lane_group >= g, win, w)
    return w


def _ffn_units(d_ff):
    return [(ci, c0, cw, u0, min(FFN_UNIT, cw - u0))
            for ci, (c0, cw) in enumerate(_ffn_chunks(d_ff)) for u0 in range(0, cw, FFN_UNIT)]


def _prompt_layer_body(
        x_ref, g1_ref, w_in_ref, pool_w_ref, pool_scale_ref, sgu_w_ref, sgu_b_ref,
        conv_w_ref, conv_b_ref, cn_g_ref, cn_b_ref, w_out_ref, g2_ref,
        w_gate_ref, w_val_ref, fconv_w_ref, fconv_b_ref, w_down_ref, fnorm_ref, zero_ref,
        xo_ref, pool_tail_ref, conv_tail_ref, ffn_tail_ref,
        z_buf, za_buf, c_buf, rot_buf, wb_buf, chain_a, chain_b, g_buf, val_buf, act_buf, mix_buf,
        x1_buf, h2_buf,
        *, tm, n_tiles, tiles_per_seq, d_a, d_b, d_c, d_ff, pool_halo, conv_halo, ffn_halo,
        conv_k, ffn_k, apply_final_norm):
    j = pl.program_id(0)
    s = lax.rem(jnp.minimum(j, n_tiles - 1), tiles_per_seq)
    o_u, o_v, o_c, o_g = d_a, d_a + d_b, d_a + 2 * d_b, d_a + 2 * d_b + d_c

    @pl.when(j == 0)
    def _no_ffn_tile_yet():
        x1_buf[...] = jnp.zeros(x1_buf.shape, F32)
        h2_buf[...] = jnp.zeros(h2_buf.shape, BF16)
        for k in range(conv_k):
            wb_buf[k * SUBLANES:(k + 1) * SUBLANES, :] = jnp.broadcast_to(conv_w_ref[k:k + 1, :], (SUBLANES, d_c))

    @pl.when(s == 0)
    def _zero_mixer_left_state():
        za_buf[0:pool_halo, :] = jnp.zeros((pool_halo, d_a), F32)
        c_buf[0:conv_halo, :] = jnp.zeros((conv_halo, d_c), F32)

    @pl.when((j == 0) | (lax.rem(j + tiles_per_seq - 1, tiles_per_seq) == 0))
    def _zero_ffn_left_state():
        g_buf[0:ffn_halo, :] = jnp.zeros((ffn_halo, d_ff), F32)

    def mix_project():
        x = x_ref[...]
        h = (x * _rms_scale(x) * g1_ref[...]).astype(BF16)
        z_buf[...] = _dot(h, w_in_ref[...])

    def mix_pool_glu():
        _prompt_pool(z_buf, za_buf, mix_buf, pool_tail_ref, s, tm=tm, d_a=d_a, pool_halo=pool_halo)
        c_buf[conv_halo:conv_halo + tm, :] = z_buf[:, o_c:o_g] * jax.nn.sigmoid(z_buf[:, o_g:o_g + d_c])

    def mix_sgu():
        n_hd = SGU_HEADS * SGU_CHUNK
        w_row = lax.broadcasted_iota(jnp.int32, (n_hd, SGU_CHUNK), 0) % SGU_CHUNK
        w_col = lax.broadcasted_iota(jnp.int32, (n_hd, SGU_CHUNK), 1)
        w_tril = jnp.where(w_col <= w_row, sgu_w_ref[...], 0.0).astype(BF16)
        head_of_lane = lax.broadcasted_iota(jnp.int32, (SGU_CHUNK, d_b), 1) // (d_b // SGU_HEADS)
        for r0 in range(0, tm, SGU_CHUNK):
            v = z_buf[r0:r0 + SGU_CHUNK, o_v:o_c].astype(BF16)
            per_head = _dot(w_tril, v)
            gate = per_head[0:SGU_CHUNK]
            for hd in range(1, SGU_HEADS):
                gate = jnp.where(head_of_lane >= hd,
                                 per_head[hd * SGU_CHUNK:(hd + 1) * SGU_CHUNK], gate)
            b_out = z_buf[r0:r0 + SGU_CHUNK, o_u:o_v] * (gate + sgu_b_ref[...])
            mix_buf[r0:r0 + SGU_CHUNK, d_a:d_a + d_b] = b_out.astype(BF16)

    def mix_conv(row_lo, row_hi):
        window = c_buf[row_lo:row_hi + conv_halo, :]
        for r in range(1, SUBLANES):
            rot_buf[r - 1] = pltpu.roll(window, r, axis=0)
        chain_offset = pl.multiple_of(zero_ref[0], SUBLANES)
        for chain in (chain_a, chain_b):
            chain[0:1, :] = conv_b_ref[...]
        for bi, r0 in enumerate(range(0, row_hi - row_lo, ROW_BLOCK)):
            chain = (chain_a, chain_b)[bi % 2]
            acc = jnp.broadcast_to(chain[pl.ds(chain_offset, 1), :], (ROW_BLOCK, d_c))
            for k in range(conv_k):
                off = conv_halo - (conv_k - 1) + k
                r = -off % SUBLANES
                lo = off + r + r0
                tap = (c_buf[row_lo + lo:row_lo + lo + ROW_BLOCK, :] if r == 0
                       else rot_buf[r - 1, lo:lo + ROW_BLOCK, :])
                w_k = wb_buf[k * SUBLANES:(k + 1) * SUBLANES, :]
                acc = acc + jnp.concatenate([w_k] * (ROW_BLOCK // SUBLANES), axis=0) * tap
            c_out = _layernorm_silu(acc, cn_g_ref[...], cn_b_ref[...])
            mix_buf[row_lo + r0:row_lo + r0 + ROW_BLOCK, d_a + d_b:d_a + d_b + d_c] = c_out.astype(BF16)
            chain[pl.ds(chain_offset + SUBLANES, SUBLANES), :] = c_out[0:SUBLANES, :]

    def mix_output():
        conv_tail_ref[...] = c_buf[tm:tm + conv_halo, :]
        c_buf[0:conv_halo, :] = c_buf[tm:tm + conv_halo, :]
        mixed = _dot(mix_buf[:, 0:d_a], pool_w_ref[...]) * pool_scale_ref[...]
        mix_buf[:, 0:d_a] = mixed.astype(BF16)
        x1 = x_ref[...] + _dot(mix_buf[...], w_out_ref[...])
        x1_buf[...] = x1
        h2_buf[...] = (x1 * _rms_scale(x1) * g2_ref[...]).astype(BF16)

    def ffn_dots(unit):
        _, c0, _, u0, uw = unit
        cols = slice(c0 + u0, c0 + u0 + uw)
        g_buf[ffn_halo:ffn_halo + tm, cols] = _dot(h2_buf[...], w_gate_ref[:, cols])
        val_buf[:, cols] = _dot(h2_buf[...], w_val_ref[:, cols])

    def ffn_gate(unit):
        ci, c0, _, u0, uw = unit
        act_cur = act_buf.at[ci % 2]
        for r0 in range(0, tm, FFN_ROWS):
            for l0 in range(0, uw, FFN_LANES):
                cols = slice(c0 + u0 + l0, c0 + u0 + l0 + FFN_LANES)
                g_win = g_buf[r0:r0 + ffn_halo + FFN_ROWS, cols]
                gate_c = jnp.broadcast_to(fconv_b_ref[:, cols], (FFN_ROWS, FFN_LANES))
                for k in range(ffn_k):
                    shift = ffn_k - 1 - k
                    tap = g_win if shift == 0 else pltpu.roll(g_win, shift, axis=0)
                    gate_c = gate_c + fconv_w_ref[k:k + 1, cols] * tap[ffn_halo:]
                act = _silu(gate_c) * val_buf[r0:r0 + FFN_ROWS, cols]
                act_cur[r0:r0 + FFN_ROWS, u0 + l0:u0 + l0 + FFN_LANES] = act.astype(BF16)
        cols = slice(c0 + u0, c0 + u0 + uw)
        ffn_tail_ref[:, cols] = g_buf[tm:tm + ffn_halo, cols]
        g_buf[0:ffn_halo, cols] = g_buf[tm:tm + ffn_halo, cols]

    chunks = _ffn_chunks(d_ff)

    def ffn_down(ci):
        c0, cw = chunks[ci]
        part = _dot(act_buf.at[ci % 2][:, 0:cw], w_down_ref[c0:c0 + cw, :])
        if ci == 0:
            xo_ref[...] = x1_buf[...] + part
        elif ci < len(chunks) - 1:
            xo_ref[...] += part
        else:
            x2 = xo_ref[...] + part
            if apply_final_norm:
                x2 = x2 * _rms_scale(x2) * fnorm_ref[...]
            xo_ref[...] = x2

    units = _ffn_units(d_ff)
    assert [u[0] for u in units] == [0, 0, 1, 1, 2, 2]
    ffn_dots(units[0])
    mix_project()
    ffn_dots(units[1])
    ffn_dots(units[2])
    ffn_dots(units[3])
    mix_pool_glu()
    ffn_gate(units[0])
    ffn_gate(units[1])
    mix_sgu()
    ffn_dots(units[4])
    ffn_down(0)
    mix_conv(0, tm // 2)
    mix_conv(tm // 2, tm)
    ffn_gate(units[2])
    ffn_gate(units[3])
    ffn_dots(units[5])
    ffn_down(1)
    mix_output()
    ffn_gate(units[4])
    ffn_gate(units[5])
    ffn_down(2)


def _prompt_pool(z_buf, za_buf, mix_buf, pool_tail_ref, s, *, tm, d_a, pool_halo):
    pool_gc = d_a // len(POOL_WINDOWS)
    pool_max = max(POOL_WINDOWS)
    za_buf[pool_halo:pool_halo + tm, :] = z_buf[:, 0:d_a]
    lane_group = lax.broadcasted_iota(jnp.int32, (POOL_BLOCK, d_a), 1) // pool_gc
    lane_w = _lane_window(lane_group)
    row = lax.broadcasted_iota(jnp.int32, (POOL_BLOCK, d_a), 0)
    for r0 in range(0, tm, POOL_BLOCK):
        sums = {1: za_buf[r0:r0 + pool_halo + POOL_BLOCK, :]}
        w = 1
        while w < pool_max:
            sums[2 * w] = sums[w] + pltpu.roll(sums[w], w, axis=0)
            w *= 2
        snapshots = {w: sums[w][pool_halo:] for w in POOL_WINDOWS}
        cnt = jnp.minimum(s * tm + r0 + row + 1, lane_w).astype(F32)
        pooled = _pool_window_select(snapshots, lane_group) / cnt - sums[1][pool_halo:]
        mix_buf[r0:r0 + POOL_BLOCK, 0:d_a] = pooled.astype(BF16)
    pool_tail_ref[...] = za_buf[tm:tm + pool_halo, :]
    za_buf[0:pool_halo, :] = za_buf[tm:tm + pool_halo, :]


_LAYER_WEIGHTS = ['norm1', 'w_in', 'pool_w', 'pool_scale', 'sgu_w', 'sgu_b', 'conv_w', 'conv_b',
                  'cnorm_g', 'cnorm_b', 'w_out', 'norm2', 'w_gate', 'w_val', 'ffn_conv_w',
                  'ffn_conv_b', 'w_down']


_UP_HALF = {'w_gate': 0, 'w_val': 1}


def _layer_operands(wts, names):
    return [wts['w_up'] if n in _UP_HALF else wts[n] for n in names]


def _weight_spec(name, arr, layer_of):
    shape, index = arr.shape[1:], (0,) * (arr.ndim - 1)
    if name in _UP_HALF:
        shape, index = shape[:-1] + (shape[-1] // 2,), index[:-1] + (_UP_HALF[name],)
    return pl.BlockSpec((None,) + shape, lambda *g: (layer_of(*g),) + index,
                        pipeline_mode=pl.Buffered(1))


def _prompt_layer(x, layer, wts, *, apply_final_norm):
    n_b, n_s, d = x.shape
    tm = PROMPT_ROWS
    d_a, d_c, d_ff = wts['pool_w'].shape[-1], wts['conv_w'].shape[-1], wts['w_down'].shape[1]
    d_b = wts['sgu_b'].shape[-1]
    conv_k, ffn_k = wts['conv_w'].shape[1], wts['ffn_conv_w'].shape[1]
    pool_halo = _round_up(max(POOL_WINDOWS) - 1, SUBLANES)
    conv_halo = _round_up(conv_k - 1, SUBLANES)
    ffn_halo = _round_up(ffn_k - 1, SUBLANES)
    assert n_s % tm == 0 and tm % SGU_CHUNK == 0 and (tm // 2) % ROW_BLOCK == 0
    assert tm % POOL_BLOCK == 0 and tm % FFN_ROWS == 0 and d_ff % FFN_LANES == 0
    assert all(w & (w - 1) == 0 for w in POOL_WINDOWS)

    tiles_per_seq = n_s // tm
    n_tiles = n_b * tiles_per_seq

    def mixer_tile(j):
        t = jnp.minimum(j, n_tiles - 1)
        return t // tiles_per_seq, t % tiles_per_seq

    def ffn_tile(j):
        t = jnp.maximum(j - 1, 0)
        return t // tiles_per_seq, t % tiles_per_seq

    operands = _layer_operands(wts, _LAYER_WEIGHTS)
    in_specs = ([pl.BlockSpec((None, tm, d), lambda j: (*mixer_tile(j), 0))]
                + [_weight_spec(n, a, lambda j: layer) for n, a in zip(_LAYER_WEIGHTS, operands)]
                + [pl.BlockSpec(wts['final_norm'].shape, lambda j: (0, 0),
                                pipeline_mode=pl.Buffered(1)),
                   pl.BlockSpec(memory_space=pltpu.SMEM)])
    out_shape = [jax.ShapeDtypeStruct((n_b, n_s, d), F32),
                 jax.ShapeDtypeStruct((n_b, pool_halo, d_a), F32),
                 jax.ShapeDtypeStruct((n_b, conv_halo, d_c), F32),
                 jax.ShapeDtypeStruct((n_b, ffn_halo, d_ff), F32)]
    out_specs = [pl.BlockSpec((None, tm, d), lambda j: (*ffn_tile(j), 0)),
                 pl.BlockSpec((None, pool_halo, d_a), lambda j: (mixer_tile(j)[0], 0, 0)),
                 pl.BlockSpec((None, conv_halo, d_c), lambda j: (mixer_tile(j)[0], 0, 0)),
                 pl.BlockSpec((None, ffn_halo, d_ff), lambda j: (ffn_tile(j)[0], 0, 0))]
    cw_max = min(FFN_CHUNK, d_ff)
    scratch = [pltpu.VMEM((tm, wts['w_in'].shape[-1]), F32),
               pltpu.VMEM((pool_halo + tm, d_a), F32),
               pltpu.VMEM((conv_halo + tm, d_c), F32),
               pltpu.VMEM((SUBLANES - 1, conv_halo + tm // 2, d_c), F32),
               pltpu.VMEM((conv_k * SUBLANES, d_c), F32),
               pltpu.VMEM((2 * SUBLANES, d_c), F32),
               pltpu.VMEM((2 * SUBLANES, d_c), F32),
               pltpu.VMEM((ffn_halo + tm, d_ff), F32),
               pltpu.VMEM((tm, d_ff), F32),
               pltpu.VMEM((2, tm, cw_max), BF16),
               pltpu.VMEM((tm, d), BF16),
               pltpu.VMEM((tm, d), F32),
               pltpu.VMEM((tm, d), BF16)]
    body = functools.partial(
        _prompt_layer_body, tm=tm, n_tiles=n_tiles, tiles_per_seq=tiles_per_seq,
        d_a=d_a, d_b=d_b, d_c=d_c, d_ff=d_ff,
        pool_halo=pool_halo, conv_halo=conv_halo, ffn_halo=ffn_halo,
        conv_k=conv_k, ffn_k=ffn_k, apply_final_norm=apply_final_norm)
    return pl.pallas_call(
        body, grid=(n_tiles + 1,), in_specs=in_specs, out_specs=out_specs,
        out_shape=out_shape, scratch_shapes=scratch, name=f'prompt_layer_{layer}',
        compiler_params=pltpu.CompilerParams(
            dimension_semantics=('arbitrary',), vmem_limit_bytes=VMEM_LIMIT_BYTES),
    )(x, *operands, wts['final_norm'], jnp.zeros((1,), jnp.int32))


def _shift_in_rows(prev_ref, new_ref, n_new):
    n_state = prev_ref.shape[0]
    for j in range(n_state - n_new):
        new_ref[j] = prev_ref[j + n_new]


def _sample_body(
        x_ref, pool_st_ref, conv_st_ref, ffn_st_ref,
        g1_ref, w_in_ref, pool_w_ref, pool_scale_ref, sgu_w_ref, sgu_b_ref,
        conv_w_ref, conv_b_ref, cn_g_ref, cn_b_ref, w_out_ref, g2_ref,
        w_gate_ref, w_val_ref, fconv_w_ref, fconv_b_ref, w_down_ref, fnorm_ref,
        xo_ref, pool_new_ref, conv_new_ref, ffn_new_ref, v_ref,
        x_all, z_buf, g_buf, mix_buf, acc_buf,
        *, depth, n_t, n_seq, d_a, d_b, d_c, d_ff, conv_k, ffn_k):
    layer, blk = pl.program_id(0), pl.program_id(1)
    pool_gc = d_a // len(POOL_WINDOWS)
    pool_max = max(POOL_WINDOWS)
    n_rows = n_t * n_seq
    n_pool, n_conv = pool_max - 1, conv_k - 1

    def slab(t):
        return slice(t * n_seq, (t + 1) * n_seq)

    @pl.when(layer == 0)
    def _load_tokens():
        for t in range(n_t):
            x_all[blk, slab(t), :] = x_ref[t]

    x = x_all[blk]
    h = (x * _rms_scale(x) * g1_ref[...]).astype(BF16)
    z_buf[...] = _dot(h, w_in_ref[...])
    o_u, o_v, o_c, o_g = d_a, d_a + d_b, d_a + 2 * d_b, d_a + 2 * d_b + d_c
    _shift_in_rows(pool_st_ref, pool_new_ref, n_t)
    _shift_in_rows(conv_st_ref, conv_new_ref, n_t)
    for t in range(n_t):
        pool_new_ref[n_pool - n_t + t] = z_buf[slab(t), 0:d_a]
        v_ref[t] = z_buf[slab(t), o_v:o_c]

    lane_group = lax.broadcasted_iota(jnp.int32, (n_seq, d_a), 1) // pool_gc
    lane_w = _lane_window(lane_group)

    def pool_row(j):
        return pool_st_ref[j] if j < n_pool else z_buf[slab(j - n_pool), 0:d_a]

    for t in range(n_t):
        cur = pool_row(n_pool + t)
        acc = cur
        snapshots = {}
        for i in range(1, pool_max):
            acc = acc + pool_row(n_pool + t - i)
            if i + 1 in POOL_WINDOWS:
                snapshots[i + 1] = acc
        cnt = jnp.minimum(PAST_LEN + t + 1, lane_w).astype(F32)
        pooled = _pool_window_select(snapshots, lane_group) / cnt - cur
        mix_buf[slab(t), 0:d_a] = pooled.astype(BF16)
    mixed = _dot(mix_buf[:, 0:d_a], pool_w_ref[...]) * pool_scale_ref[...]
    mix_buf[:, 0:d_a] = mixed.astype(BF16)

    for t in range(n_t):
        gate = jnp.broadcast_to(sgu_b_ref[t:t + 1, :], (n_seq, d_b))
        for sp in range(t + 1):
            gate = gate + sgu_w_ref[t * n_t + sp:t * n_t + sp + 1, :] * z_buf[slab(sp), o_v:o_c]
        mix_buf[slab(t), d_a:d_a + d_b] = (z_buf[slab(t), o_u:o_v] * gate).astype(BF16)

    for t in range(n_t):
        conv_new_ref[n_conv - n_t + t] = (z_buf[slab(t), o_c:o_g]
                                          * jax.nn.sigmoid(z_buf[slab(t), o_g:o_g + d_c]))

    def conv_row(j):
        return conv_st_ref[j] if j < n_conv else conv_new_ref[j - n_t]

    for t in range(n_t):
        acc = jnp.broadcast_to(conv_b_ref[...], (n_seq, d_c))
        for k in range(conv_k):
            acc = acc + conv_w_ref[k:k + 1, :] * conv_row(t + k)
        c_out = _layernorm_silu(acc, cn_g_ref[...], cn_b_ref[...])
        mix_buf[slab(t), d_a + d_b:d_a + d_b + d_c] = c_out.astype(BF16)

    x1 = x + _dot(mix_buf[...], w_out_ref[...])
    acc_buf[...] = x1

    h2 = (x1 * _rms_scale(x1) * g2_ref[...]).astype(BF16)
    n_keep = ffn_k - 1
    for c0, cw in _ffn_chunks(d_ff):
        for k in range(n_keep):
            g_buf[slab(k), 0:cw] = ffn_st_ref[k, :, c0:c0 + cw]
        g_buf[n_keep * n_seq:n_keep * n_seq + n_rows, 0:cw] = _dot(h2, w_gate_ref[:, c0:c0 + cw])
        for k in range(n_keep):
            ffn_new_ref[k, :, c0:c0 + cw] = g_buf[slab(n_t + k), 0:cw]
        val = _dot(h2, w_val_ref[:, c0:c0 + cw])
        gate_c = jnp.broadcast_to(fconv_b_ref[:, c0:c0 + cw], (n_rows, cw))
        for k in range(ffn_k):
            gate_c = gate_c + fconv_w_ref[k:k + 1, c0:c0 + cw] * g_buf[k * n_seq:k * n_seq + n_rows, 0:cw]
        act = (_silu(gate_c) * val).astype(BF16)
        acc_buf[...] += _dot(act, w_down_ref[c0:c0 + cw, :])

    x2 = acc_buf[...]
    x_all[blk] = x2

    @pl.when(layer == depth - 1)
    def _emit_tokens():
        y = x2 * _rms_scale(x2) * fnorm_ref[...]
        for t in range(n_t):
            xo_ref[t] = y[slab(t)]


def _sample_trunk(x, pool_st, conv_st, ffn_st, wts):
    n_t, n_all, d = x.shape
    depth = pool_st.shape[0]
    n_seq = min(SAMPLE_SEQS, n_all)
    n_blocks = n_all // n_seq
    d_a, d_c, d_ff = wts['pool_w'].shape[-1], wts['conv_w'].shape[-1], wts['w_down'].shape[1]
    d_b = wts['sgu_b'].shape[-1]
    conv_k, ffn_k = wts['conv_w'].shape[1], wts['ffn_conv_w'].shape[1]
    n_keep = ffn_k - 1
    assert n_all % n_seq == 0 and n_seq % SUBLANES == 0
    assert n_keep <= n_t <= min(conv_k - 1, max(POOL_WINDOWS) - 1)

    def state_spec(n_time, ch):
        return pl.BlockSpec((None, n_time, n_seq, ch), lambda l, i: (l, 0, i, 0))

    names = [n if n not in ('sgu_w', 'sgu_b') else n + '_new' for n in _LAYER_WEIGHTS]
    operands = _layer_operands(wts, names)
    tokens_spec = pl.BlockSpec((n_t, n_seq, d), lambda l, i: (0, i, 0))
    in_specs = ([tokens_spec, state_spec(pool_st.shape[1], d_a), state_spec(conv_st.shape[1], d_c),
                 state_spec(ffn_st.shape[1], d_ff)]
                + [_weight_spec(n, a, lambda l, i: l) for n, a in zip(names, operands)]
                + [pl.BlockSpec(wts['final_norm'].shape, lambda l, i: (0, 0),
                                pipeline_mode=pl.Buffered(1))])
    state_dims = [(pool_st.shape[1], d_a), (conv_st.shape[1], d_c), (n_keep, d_ff), (n_t, d_b)]
    out_shape = ([jax.ShapeDtypeStruct((n_t, n_all, d), F32)]
                 + [jax.ShapeDtypeStruct((depth, nt, n_all, ch), F32) for nt, ch in state_dims])
    y_spec = pl.BlockSpec((n_t, n_seq, d), lambda l, i: (0, jnp.where(l == depth - 1, i, 0), 0))
    out_specs = [y_spec] + [state_spec(nt, ch) for nt, ch in state_dims]
    n_rows = n_t * n_seq
    cw_max = min(FFN_CHUNK, d_ff)
    scratch = [pltpu.VMEM((n_blocks, n_rows, d), F32),
               pltpu.VMEM((n_rows, wts['w_in'].shape[-1]), F32),
               pltpu.VMEM((n_keep * n_seq + n_rows, cw_max), F32),
               pltpu.VMEM((n_rows, d), BF16),
               pltpu.VMEM((n_rows, d), F32)]
    body = functools.partial(
        _sample_body, depth=depth, n_t=n_t, n_seq=n_seq, d_a=d_a, d_b=d_b, d_c=d_c, d_ff=d_ff,
        conv_k=conv_k, ffn_k=ffn_k)
    return pl.pallas_call(
        body, grid=(depth, n_blocks), in_specs=in_specs, out_specs=out_specs,
        out_shape=out_shape, scratch_shapes=scratch, name='sample_trunk',
        compiler_params=pltpu.CompilerParams(
            dimension_semantics=('arbitrary', 'arbitrary'), vmem_limit_bytes=VMEM_LIMIT_BYTES),
    )(x, pool_st, conv_st, ffn_st, *operands, wts['final_norm'])


def _block_diag(w):
    depth, n_g, c, _ = w.shape
    eye = jnp.eye(n_g, dtype=w.dtype)
    return jnp.einsum('lgcd,gh->lgchd', w, eye).reshape(depth, n_g * c, n_g * c)


def kernel(x_prompt, x_sample, state_pool, state_conv, state_ffn_conv, norm1, w_in, pool_w, pool_scale, sgu_w, sgu_b, conv_w, conv_b, cnorm_g, cnorm_b, w_out, norm2, w_up, ffn_conv_w, ffn_conv_b, w_down, final_norm):
    depth = norm1.shape[0]
    d_ff = w_down.shape[1]
    d_b = (w_in.shape[-1] - pool_scale.shape[-1] - 2 * conv_w.shape[-1]) // 2
    hd = d_b // SGU_HEADS
    n_t = x_sample.shape[1]

    row = lambda a: a[:, None, :]
    wts = dict(
        norm1=row(norm1), w_in=w_in.astype(BF16),
        pool_w=_block_diag(pool_w).astype(BF16), pool_scale=row(pool_scale),
        sgu_w=sgu_w.reshape(depth, SGU_HEADS * SGU_CHUNK, SGU_CHUNK),
        sgu_b=jnp.repeat(jnp.swapaxes(sgu_b, 1, 2), hd, axis=-1),
        conv_w=conv_w, conv_b=row(conv_b), cnorm_g=row(cnorm_g), cnorm_b=row(cnorm_b),
        w_out=w_out.astype(BF16), norm2=row(norm2),
        w_up=w_up.astype(BF16),
        ffn_conv_w=ffn_conv_w, ffn_conv_b=row(ffn_conv_b), w_down=w_down.astype(BF16),
        final_norm=final_norm[None, :])
    w_new = jnp.transpose(sgu_w[:, :, :n_t, :n_t], (0, 2, 3, 1)).reshape(depth, n_t * n_t, SGU_HEADS)
    wts['sgu_w_new'] = jnp.repeat(w_new, hd, axis=-1)
    wts['sgu_b_new'] = jnp.repeat(jnp.swapaxes(sgu_b[:, :, :n_t], 1, 2), hd, axis=-1)

    x = x_prompt
    pool_p, conv_p, ffn_p = [], [], []
    for l in range(depth):
        x, pt, ct, ft = _prompt_layer(x, l, wts, apply_final_norm=(l == depth - 1))
        pool_p.append(pt[:, pt.shape[1] - (max(POOL_WINDOWS) - 1):])
        conv_p.append(ct[:, ct.shape[1] - (conv_w.shape[1] - 1):])
        ffn_p.append(ft[:, ft.shape[1] - (ffn_conv_w.shape[1] - 1):])
    y_prompt = x

    swap = lambda a: jnp.swapaxes(a, -3, -2)
    ys, pool_s, conv_s, ffn_s, v_s = _sample_trunk(
        swap(x_sample), swap(state_pool), swap(state_conv), swap(state_ffn_conv), wts)

    return (y_prompt, swap(ys),
            jnp.stack(pool_p), swap(pool_s),
            jnp.stack(conv_p), swap(conv_s),
            jnp.stack(ffn_p), swap(ffn_s),
            swap(v_s))
```

```python
import functools

import jax
import jax.numpy as jnp
from jax import lax
from jax.experimental import pallas as pl
from jax.experimental.pallas import tpu as pltpu

POOL_WINDOWS = (2, 4, 8, 16)
SGU_HEADS = 4
SGU_CHUNK = 128
PAST_LEN = 16384
EPS = 1e-6

LANES = 128
SUBLANES = 8
MXU_DIM = 256
VMEM_LIMIT_BYTES = 60 * 1024 * 1024

PROMPT_ROWS = 512
SAMPLE_SEQS = 64
ROW_BLOCK = 32
POOL_BLOCK = 128
FFN_CHUNK = 4 * MXU_DIM
FFN_UNIT = 2 * MXU_DIM
FFN_ROWS = 128
FFN_LANES = 2 * LANES

F32 = jnp.float32
BF16 = jnp.bfloat16


def _round_up(n, m):
    return -(-n // m) * m


def _rms_scale(x):
    return lax.rsqrt(jnp.mean(x * x, axis=-1, keepdims=True) + EPS)


def _silu(x):
    return x * jax.nn.sigmoid(x)


def _layernorm_silu(y, g, b):
    mu = jnp.mean(y, axis=-1, keepdims=True)
    d = y - mu
    var = jnp.mean(d * d, axis=-1, keepdims=True)
    return _silu(d * lax.rsqrt(var + EPS) * g + b)


def _dot(a, b):
    return jnp.dot(a, b, preferred_element_type=F32)


def _ffn_chunks(d_ff):
    return [(c0, min(FFN_CHUNK, d_ff - c0)) for c0 in range(0, d_ff, FFN_CHUNK)]


def _pool_window_select(snapshots, lane_group):
    out = snapshots[POOL_WINDOWS[0]]
    for g, w in enumerate(POOL_WINDOWS[1:], start=1):
        out = jnp.where(lane_group >= g, snapshots[w], out)
    return out


def _lane_window(lane_group):
    w = jnp.full(lane_group.shape, POOL_WINDOWS[0], jnp.int32)
    for g, win in enumerate(POOL_WINDOWS[1:], start=1):
        w = jnp.where(lane_group >= g, win, w)
    return w


def _ffn_units(d_ff):
    return [(ci, c0, cw, u0, min(FFN_UNIT, cw - u0))
            for ci, (c0, cw) in enumerate(_ffn_chunks(d_ff)) for u0 in range(0, cw, FFN_UNIT)]


def _prompt_layer_body(
        x_ref, g1_ref, w_in_ref, pool_w_ref, pool_scale_ref, sgu_w_ref, sgu_b_ref,
        conv_w_ref, conv_b_ref, cn_g_ref, cn_b_ref, w_out_ref, g2_ref,
        w_gate_ref, w_val_ref, fconv_w_ref, fconv_b_ref, w_down_ref, fnorm_ref, zero_ref,
        xo_ref, pool_tail_ref, conv_tail_ref, ffn_tail_ref,
        z_buf, za_buf, c_buf, rot_buf, wb_buf, g_buf, val_buf, act_buf, mix_buf, x1_buf, h2_buf,
        *, tm, n_tiles, tiles_per_seq, d_a, d_b, d_c, d_ff, pool_halo, conv_halo, ffn_halo,
        conv_k, ffn_k, apply_final_norm):
    j = pl.program_id(0)
    s = lax.rem(jnp.minimum(j, n_tiles - 1), tiles_per_seq)
    o_u, o_v, o_c, o_g = d_a, d_a + d_b, d_a + 2 * d_b, d_a + 2 * d_b + d_c

    @pl.when(j == 0)
    def _no_ffn_tile_yet():
        x1_buf[...] = jnp.zeros(x1_buf.shape, F32)
        h2_buf[...] = jnp.zeros(h2_buf.shape, BF16)
        for k in range(conv_k):
            wb_buf[k * SUBLANES:(k + 1) * SUBLANES, :] = jnp.broadcast_to(conv_w_ref[k:k + 1, :], (SUBLANES, d_c))

    @pl.when(s == 0)
    def _zero_mixer_left_state():
        za_buf[0:pool_halo, :] = jnp.zeros((pool_halo, d_a), F32)
        c_buf[0:conv_halo, :] = jnp.zeros((conv_halo, d_c), F32)

    @pl.when((j == 0) | (lax.rem(j + tiles_per_seq - 1, tiles_per_seq) == 0))
    def _zero_ffn_left_state():
        g_buf[0:ffn_halo, :] = jnp.zeros((ffn_halo, d_ff), F32)

    def mix_project():
        x = x_ref[...]
        h = (x * _rms_scale(x) * g1_ref[...]).astype(BF16)
        z_buf[...] = _dot(h, w_in_ref[...])

    def mix_pool_glu():
        _prompt_pool(z_buf, za_buf, mix_buf, pool_tail_ref, s, tm=tm, d_a=d_a, pool_halo=pool_halo)
        c_buf[conv_halo:conv_halo + tm, :] = z_buf[:, o_c:o_g] * jax.nn.sigmoid(z_buf[:, o_g:o_g + d_c])

    def mix_sgu():
        n_hd = SGU_HEADS * SGU_CHUNK
        w_row = lax.broadcasted_iota(jnp.int32, (n_hd, SGU_CHUNK), 0) % SGU_CHUNK
        w_col = lax.broadcasted_iota(jnp.int32, (n_hd, SGU_CHUNK), 1)
        w_tril = jnp.where(w_col <= w_row, sgu_w_ref[...], 0.0).astype(BF16)
        head_of_lane = lax.broadcasted_iota(jnp.int32, (SGU_CHUNK, d_b), 1) // (d_b // SGU_HEADS)
        for r0 in range(0, tm, SGU_CHUNK):
            v = z_buf[r0:r0 + SGU_CHUNK, o_v:o_c].astype(BF16)
            per_head = _dot(w_tril, v)
            gate = per_head[0:SGU_CHUNK]
            for hd in range(1, SGU_HEADS):
                gate = jnp.where(head_of_lane >= hd,
                                 per_head[hd * SGU_CHUNK:(hd + 1) * SGU_CHUNK], gate)
            b_out = z_buf[r0:r0 + SGU_CHUNK, o_u:o_v] * (gate + sgu_b_ref[...])
            mix_buf[r0:r0 + SGU_CHUNK, d_a:d_a + d_b] = b_out.astype(BF16)

    gate_results = {}

    def mix_conv(row_lo, row_hi, anchor_units):
        window = c_buf[row_lo:row_hi + conv_halo, :]
        n_win = row_hi - row_lo + conv_halo
        for r in range(1, SUBLANES):
            rot_buf[r - 1, 0:n_win, :] = pltpu.roll(window, r, axis=0)
        start = pl.multiple_of(zero_ref[0], SUBLANES)
        n_blocks = (row_hi - row_lo) // ROW_BLOCK
        for bi, r0 in enumerate(range(0, row_hi - row_lo, ROW_BLOCK)):
            per_unit = n_blocks // len(anchor_units)
            anchor = gate_results[anchor_units[bi // per_unit]]
            a_row = (bi % per_unit) * (tm // per_unit)
            rot_buf[0, pl.ds(start + n_win, SUBLANES), :] = anchor[a_row:a_row + SUBLANES, 0:d_c]
            acc = jnp.broadcast_to(conv_b_ref[...], (ROW_BLOCK, d_c))
            for k in range(conv_k):
                off = conv_halo - (conv_k - 1) + k
                r = -off % SUBLANES
                lo = off + r + r0
                tap = (c_buf[row_lo + lo:row_lo + lo + ROW_BLOCK, :] if r == 0
                       else rot_buf[r - 1, pl.ds(start + lo, ROW_BLOCK), :])
                w_k = wb_buf[k * SUBLANES:(k + 1) * SUBLANES, :]
                acc = acc + jnp.concatenate([w_k] * (ROW_BLOCK // SUBLANES), axis=0) * tap
            c_out = _layernorm_silu(acc, cn_g_ref[...], cn_b_ref[...])
            mix_buf[row_lo + r0:row_lo + r0 + ROW_BLOCK, d_a + d_b:d_a + d_b + d_c] = c_out.astype(BF16)

    def mix_output():
        conv_tail_ref[...] = c_buf[tm:tm + conv_halo, :]
        c_buf[0:conv_halo, :] = c_buf[tm:tm + conv_halo, :]
        mixed = _dot(mix_buf[:, 0:d_a], pool_w_ref[...]) * pool_scale_ref[...]
        mix_buf[:, 0:d_a] = mixed.astype(BF16)
        x1 = x_ref[...] + _dot(mix_buf[...], w_out_ref[...])
        x1_buf[...] = x1
        h2_buf[...] = (x1 * _rms_scale(x1) * g2_ref[...]).astype(BF16)

    def ffn_dots(unit):
        _, c0, _, u0, uw = unit
        cols = slice(c0 + u0, c0 + u0 + uw)
        gate_results[c0 + u0] = _dot(h2_buf[...], w_gate_ref[:, cols])
        g_buf[ffn_halo:ffn_halo + tm, cols] = gate_results[c0 + u0]
        val_buf[:, cols] = _dot(h2_buf[...], w_val_ref[:, cols])

    def ffn_gate(unit):
        ci, c0, _, u0, uw = unit
        act_cur = act_buf.at[ci % 2]
        for r0 in range(0, tm, FFN_ROWS):
            for l0 in range(0, uw, FFN_LANES):
                cols = slice(c0 + u0 + l0, c0 + u0 + l0 + FFN_LANES)
                g_win = g_buf[r0:r0 + ffn_halo + FFN_ROWS, cols]
                gate_c = jnp.broadcast_to(fconv_b_ref[:, cols], (FFN_ROWS, FFN_LANES))
                for k in range(ffn_k):
                    shift = ffn_k - 1 - k
                    tap = g_win if shift == 0 else pltpu.roll(g_win, shift, axis=0)
                    gate_c = gate_c + fconv_w_ref[k:k + 1, cols] * tap[ffn_halo:]
                act = _silu(gate_c) * val_buf[r0:r0 + FFN_ROWS, cols]
                act_cur[r0:r0 + FFN_ROWS, u0 + l0:u0 + l0 + FFN_LANES] = act.astype(BF16)
        cols = slice(c0 + u0, c0 + u0 + uw)
        ffn_tail_ref[:, cols] = g_buf[tm:tm + ffn_halo, cols]
        g_buf[0:ffn_halo, cols] = g_buf[tm:tm + ffn_halo, cols]

    chunks = _ffn_chunks(d_ff)

    def ffn_down(ci):
        c0, cw = chunks[ci]
        part = _dot(act_buf.at[ci % 2][:, 0:cw], w_down_ref[c0:c0 + cw, :])
        if ci == 0:
            xo_ref[...] = x1_buf[...] + part
        elif ci < len(chunks) - 1:
            xo_ref[...] += part
        else:
            x2 = xo_ref[...] + part
            if apply_final_norm:
                x2 = x2 * _rms_scale(x2) * fnorm_ref[...]
            xo_ref[...] = x2

    units = _ffn_units(d_ff)
    assert [u[0] for u in units] == [0, 0, 1, 1, 2, 2]
    ffn_dots(units[0])
    mix_project()
    ffn_dots(units[1])
    ffn_dots(units[2])
    ffn_dots(units[3])
    mix_pool_glu()
    ffn_gate(units[0])
    ffn_gate(units[1])
    mix_sgu()
    ffn_dots(units[4])
    ffn_down(0)
    unit_key = lambda u: u[1] + u[3]
    mix_conv(0, tm // 2, [unit_key(units[1]), unit_key(units[2])])
    mix_conv(tm // 2, tm, [unit_key(units[3]), unit_key(units[4])])
    ffn_gate(units[2])
    ffn_gate(units[3])
    ffn_dots(units[5])
    ffn_down(1)
    mix_output()
    ffn_gate(units[4])
    ffn_gate(units[5])
    ffn_down(2)


def _prompt_pool(z_buf, za_buf, mix_buf, pool_tail_ref, s, *, tm, d_a, pool_halo):
    pool_gc = d_a // len(POOL_WINDOWS)
    pool_max = max(POOL_WINDOWS)
    za_buf[pool_halo:pool_halo + tm, :] = z_buf[:, 0:d_a]
    lane_group = lax.broadcasted_iota(jnp.int32, (POOL_BLOCK, d_a), 1) // pool_gc
    lane_w = _lane_window(lane_group)
    row = lax.broadcasted_iota(jnp.int32, (POOL_BLOCK, d_a), 0)
    for r0 in range(0, tm, POOL_BLOCK):
        sums = {1: za_buf[r0:r0 + pool_halo + POOL_BLOCK, :]}
        w = 1
        while w < pool_max:
            sums[2 * w] = sums[w] + pltpu.roll(sums[w], w, axis=0)
            w *= 2
        snapshots = {w: sums[w][pool_halo:] for w in POOL_WINDOWS}
        cnt = jnp.minimum(s * tm + r0 + row + 1, lane_w).astype(F32)
        pooled = _pool_window_select(snapshots, lane_group) / cnt - sums[1][pool_halo:]
        mix_buf[r0:r0 + POOL_BLOCK, 0:d_a] = pooled.astype(BF16)
    pool_tail_ref[...] = za_buf[tm:tm + pool_halo, :]
    za_buf[0:pool_halo, :] = za_buf[tm:tm + pool_halo, :]


_LAYER_WEIGHTS = ['norm1', 'w_in', 'pool_w', 'pool_scale', 'sgu_w', 'sgu_b', 'conv_w', 'conv_b',
                  'cnorm_g', 'cnorm_b', 'w_out', 'norm2', 'w_gate', 'w_val', 'ffn_conv_w',
                  'ffn_conv_b', 'w_down']


_UP_HALF = {'w_gate': 0, 'w_val': 1}


def _layer_operands(wts, names):
    return [wts['w_up'] if n in _UP_HALF else wts[n] for n in names]


def _weight_spec(name, arr, layer_of):
    shape, index = arr.shape[1:], (0,) * (arr.ndim - 1)
    if name in _UP_HALF:
        shape, index = shape[:-1] + (shape[-1] // 2,), index[:-1] + (_UP_HALF[name],)
    return pl.BlockSpec((None,) + shape, lambda *g: (layer_of(*g),) + index,
                        pipeline_mode=pl.Buffered(1))


def _prompt_layer(x, layer, wts, *, apply_final_norm):
    n_b, n_s, d = x.shape
    tm = PROMPT_ROWS
    d_a, d_c, d_ff = wts['pool_w'].shape[-1], wts['conv_w'].shape[-1], wts['w_down'].shape[1]
    d_b = wts['sgu_b'].shape[-1]
    conv_k, ffn_k = wts['conv_w'].shape[1], wts['ffn_conv_w'].shape[1]
    pool_halo = _round_up(max(POOL_WINDOWS) - 1, SUBLANES)
    conv_halo = _round_up(conv_k - 1, SUBLANES)
    ffn_halo = _round_up(ffn_k - 1, SUBLANES)
    assert n_s % tm == 0 and tm % SGU_CHUNK == 0 and (tm // 2) % ROW_BLOCK == 0
    assert tm % POOL_BLOCK == 0 and tm % FFN_ROWS == 0 and d_ff % FFN_LANES == 0
    assert all(w & (w - 1) == 0 for w in POOL_WINDOWS)

    tiles_per_seq = n_s // tm
    n_tiles = n_b * tiles_per_seq

    def mixer_tile(j):
        t = jnp.minimum(j, n_tiles - 1)
        return t // tiles_per_seq, t % tiles_per_seq

    def ffn_tile(j):
        t = jnp.maximum(j - 1, 0)
        return t // tiles_per_seq, t % tiles_per_seq

    operands = _layer_operands(wts, _LAYER_WEIGHTS)
    in_specs = ([pl.BlockSpec((None, tm, d), lambda j: (*mixer_tile(j), 0))]
                + [_weight_spec(n, a, lambda j: layer) for n, a in zip(_LAYER_WEIGHTS, operands)]
                + [pl.BlockSpec(wts['final_norm'].shape, lambda j: (0, 0),
                                pipeline_mode=pl.Buffered(1)),
                   pl.BlockSpec(memory_space=pltpu.SMEM)])
    out_shape = [jax.ShapeDtypeStruct((n_b, n_s, d), F32),
                 jax.ShapeDtypeStruct((n_b, pool_halo, d_a), F32),
                 jax.ShapeDtypeStruct((n_b, conv_halo, d_c), F32),
                 jax.ShapeDtypeStruct((n_b, ffn_halo, d_ff), F32)]
    out_specs = [pl.BlockSpec((None, tm, d), lambda j: (*ffn_tile(j), 0)),
                 pl.BlockSpec((None, pool_halo, d_a), lambda j: (mixer_tile(j)[0], 0, 0)),
                 pl.BlockSpec((None, conv_halo, d_c), lambda j: (mixer_tile(j)[0], 0, 0)),
                 pl.BlockSpec((None, ffn_halo, d_ff), lambda j: (ffn_tile(j)[0], 0, 0))]
    cw_max = min(FFN_CHUNK, d_ff)
    scratch = [pltpu.VMEM((tm, wts['w_in'].shape[-1]), F32),
               pltpu.VMEM((pool_halo + tm, d_a), F32),
               pltpu.VMEM((conv_halo + tm, d_c), F32),
               pltpu.VMEM((SUBLANES - 1, conv_halo + tm // 2 + SUBLANES, d_c), F32),
               pltpu.VMEM((conv_k * SUBLANES, d_c), F32),
               pltpu.VMEM((ffn_halo + tm, d_ff), F32),
               pltpu.VMEM((tm, d_ff), F32),
               pltpu.VMEM((2, tm, cw_max), BF16),
               pltpu.VMEM((tm, d), BF16),
               pltpu.VMEM((tm, d), F32),
               pltpu.VMEM((tm, d), BF16)]
    body = functools.partial(
        _prompt_layer_body, tm=tm, n_tiles=n_tiles, tiles_per_seq=tiles_per_seq,
        d_a=d_a, d_b=d_b, d_c=d_c, d_ff=d_ff,
        pool_halo=pool_halo, conv_halo=conv_halo, ffn_halo=ffn_halo,
        conv_k=conv_k, ffn_k=ffn_k, apply_final_norm=apply_final_norm)
    return pl.pallas_call(
        body, grid=(n_tiles + 1,), in_specs=in_specs, out_specs=out_specs,
        out_shape=out_shape, scratch_shapes=scratch, name=f'prompt_layer_{layer}',
        compiler_params=pltpu.CompilerParams(
            dimension_semantics=('arbitrary',), vmem_limit_bytes=VMEM_LIMIT_BYTES),
    )(x, *operands, wts['final_norm'], jnp.zeros((1,), jnp.int32))


def _shift_in_rows(prev_ref, new_ref, n_new):
    n_state = prev_ref.shape[0]
    for j in range(n_state - n_new):
        new_ref[j] = prev_ref[j + n_new]


def _sample_body(
        x_ref, pool_st_ref, conv_st_ref, ffn_st_ref,
        g1_ref, w_in_ref, pool_w_ref, pool_scale_ref, sgu_w_ref, sgu_b_ref,
        conv_w_ref, conv_b_ref, cn_g_ref, cn_b_ref, w_out_ref, g2_ref,
        w_gate_ref, w_val_ref, fconv_w_ref, fconv_b_ref, w_down_ref, fnorm_ref,
        xo_ref, pool_new_ref, conv_new_ref, ffn_new_ref, v_ref,
        x_all, z_buf, g_buf, mix_buf, acc_buf,
        *, depth, n_t, n_seq, d_a, d_b, d_c, d_ff, conv_k, ffn_k):
    layer, blk = pl.program_id(0), pl.program_id(1)
    pool_gc = d_a // len(POOL_WINDOWS)
    pool_max = max(POOL_WINDOWS)
    n_rows = n_t * n_seq
    n_pool, n_conv = pool_max - 1, conv_k - 1

    def slab(t):
        return slice(t * n_seq, (t + 1) * n_seq)

    @pl.when(layer == 0)
    def _load_tokens():
        for t in range(n_t):
            x_all[blk, slab(t), :] = x_ref[t]

    x = x_all[blk]
    h = (x * _rms_scale(x) * g1_ref[...]).astype(BF16)
    z_buf[...] = _dot(h, w_in_ref[...])
    o_u, o_v, o_c, o_g = d_a, d_a + d_b, d_a + 2 * d_b, d_a + 2 * d_b + d_c
    _shift_in_rows(pool_st_ref, pool_new_ref, n_t)
    _shift_in_rows(conv_st_ref, conv_new_ref, n_t)
    for t in range(n_t):
        pool_new_ref[n_pool - n_t + t] = z_buf[slab(t), 0:d_a]
        v_ref[t] = z_buf[slab(t), o_v:o_c]

    lane_group = lax.broadcasted_iota(jnp.int32, (n_seq, d_a), 1) // pool_gc
    lane_w = _lane_window(lane_group)

    def pool_row(j):
        return pool_st_ref[j] if j < n_pool else z_buf[slab(j - n_pool), 0:d_a]

    for t in range(n_t):
        cur = pool_row(n_pool + t)
        acc = cur
        snapshots = {}
        for i in range(1, pool_max):
            acc = acc + pool_row(n_pool + t - i)
            if i + 1 in POOL_WINDOWS:
                snapshots[i + 1] = acc
        cnt = jnp.minimum(PAST_LEN + t + 1, lane_w).astype(F32)
        pooled = _pool_window_select(snapshots, lane_group) / cnt - cur
        mix_buf[slab(t), 0:d_a] = pooled.astype(BF16)
    mixed = _dot(mix_buf[:, 0:d_a], pool_w_ref[...]) * pool_scale_ref[...]
    mix_buf[:, 0:d_a] = mixed.astype(BF16)

    for t in range(n_t):
        gate = jnp.broadcast_to(sgu_b_ref[t:t + 1, :], (n_seq, d_b))
        for sp in range(t + 1):
            gate = gate + sgu_w_ref[t * n_t + sp:t * n_t + sp + 1, :] * z_buf[slab(sp), o_v:o_c]
        mix_buf[slab(t), d_a:d_a + d_b] = (z_buf[slab(t), o_u:o_v] * gate).astype(BF16)

    for t in range(n_t):
        conv_new_ref[n_conv - n_t + t] = (z_buf[slab(t), o_c:o_g]
                                          * jax.nn.sigmoid(z_buf[slab(t), o_g:o_g + d_c]))

    def conv_row(j):
        return conv_st_ref[j] if j < n_conv else conv_new_ref[j - n_t]

    for t in range(n_t):
        acc = jnp.broadcast_to(conv_b_ref[...], (n_seq, d_c))
        for k in range(conv_k):
            acc = acc + conv_w_ref[k:k + 1, :] * conv_row(t + k)
        c_out = _layernorm_silu(acc, cn_g_ref[...], cn_b_ref[...])
        mix_buf[slab(t), d_a + d_b:d_a + d_b + d_c] = c_out.astype(BF16)

    x1 = x + _dot(mix_buf[...], w_out_ref[...])
    acc_buf[...] = x1

    h2 = (x1 * _rms_scale(x1) * g2_ref[...]).astype(BF16)
    n_keep = ffn_k - 1
    for c0, cw in _ffn_chunks(d_ff):
        for k in range(n_keep):
            g_buf[slab(k), 0:cw] = ffn_st_ref[k, :, c0:c0 + cw]
        g_buf[n_keep * n_seq:n_keep * n_seq + n_rows, 0:cw] = _dot(h2, w_gate_ref[:, c0:c0 + cw])
        for k in range(n_keep):
            ffn_new_ref[k, :, c0:c0 + cw] = g_buf[slab(n_t + k), 0:cw]
        val = _dot(h2, w_val_ref[:, c0:c0 + cw])
        gate_c = jnp.broadcast_to(fconv_b_ref[:, c0:c0 + cw], (n_rows, cw))
        for k in range(ffn_k):
            gate_c = gate_c + fconv_w_ref[k:k + 1, c0:c0 + cw] * g_buf[k * n_seq:k * n_seq + n_rows, 0:cw]
        act = (_silu(gate_c) * val).astype(BF16)
        acc_buf[...] += _dot(act, w_down_ref[c0:c0 + cw, :])

    x2 = acc_buf[...]
    x_all[blk] = x2

    @pl.when(layer == depth - 1)
    def _emit_tokens():
        y = x2 * _rms_scale(x2) * fnorm_ref[...]
        for t in range(n_t):
            xo_ref[t] = y[slab(t)]


def _sample_trunk(x, pool_st, conv_st, ffn_st, wts):
    n_t, n_all, d = x.shape
    depth = pool_st.shape[0]
    n_seq = min(SAMPLE_SEQS, n_all)
    n_blocks = n_all // n_seq
    d_a, d_c, d_ff = wts['pool_w'].shape[-1], wts['conv_w'].shape[-1], wts['w_down'].shape[1]
    d_b = wts['sgu_b'].shape[-1]
    conv_k, ffn_k = wts['conv_w'].shape[1], wts['ffn_conv_w'].shape[1]
    n_keep = ffn_k - 1
    assert n_all % n_seq == 0 and n_seq % SUBLANES == 0
    assert n_keep <= n_t <= min(conv_k - 1, max(POOL_WINDOWS) - 1)

    def state_spec(n_time, ch):
        return pl.BlockSpec((None, n_time, n_seq, ch), lambda l, i: (l, 0, i, 0))

    names = [n if n not in ('sgu_w', 'sgu_b') else n + '_new' for n in _LAYER_WEIGHTS]
    operands = _layer_operands(wts, names)
    tokens_spec = pl.BlockSpec((n_t, n_seq, d), lambda l, i: (0, i, 0))
    in_specs = ([tokens_spec, state_spec(pool_st.shape[1], d_a), state_spec(conv_st.shape[1], d_c),
                 state_spec(ffn_st.shape[1], d_ff)]
                + [_weight_spec(n, a, lambda l, i: l) for n, a in zip(names, operands)]
                + [pl.BlockSpec(wts['final_norm'].shape, lambda l, i: (0, 0),
                                pipeline_mode=pl.Buffered(1))])
    state_dims = [(pool_st.shape[1], d_a), (conv_st.shape[1], d_c), (n_keep, d_ff), (n_t, d_b)]
    out_shape = ([jax.ShapeDtypeStruct((n_t, n_all, d), F32)]
                 + [jax.ShapeDtypeStruct((depth, nt, n_all, ch), F32) for nt, ch in state_dims])
    y_spec = pl.BlockSpec((n_t, n_seq, d), lambda l, i: (0, jnp.where(l == depth - 1, i, 0), 0))
    out_specs = [y_spec] + [state_spec(nt, ch) for nt, ch in state_dims]
    n_rows = n_t * n_seq
    cw_max = min(FFN_CHUNK, d_ff)
    scratch = [pltpu.VMEM((n_blocks, n_rows, d), F32),
               pltpu.VMEM((n_rows, wts['w_in'].shape[-1]), F32),
               pltpu.VMEM((n_keep * n_seq + n_rows, cw_max), F32),
               pltpu.VMEM((n_rows, d), BF16),
               pltpu.VMEM((n_rows, d), F32)]
    body = functools.partial(
        _sample_body, depth=depth, n_t=n_t, n_seq=n_seq, d_a=d_a, d_b=d_b, d_c=d_c, d_ff=d_ff,
        conv_k=conv_k, ffn_k=ffn_k)
    return pl.pallas_call(
        body, grid=(depth, n_blocks), in_specs=in_specs, out_specs=out_specs,
        out_shape=out_shape, scratch_shapes=scratch, name='sample_trunk',
        compiler_params=pltpu.CompilerParams(
            dimension_semantics=('arbitrary', 'arbitrary'), vmem_limit_bytes=VMEM_LIMIT_BYTES),
    )(x, pool_st, conv_st, ffn_st, *operands, wts['final_norm'])


def _block_diag(w):
    depth, n_g, c, _ = w.shape
    eye = jnp.eye(n_g, dtype=w.dtype)
    return jnp.einsum('lgcd,gh->lgchd', w, eye).reshape(depth, n_g * c, n_g * c)


def kernel(x_prompt, x_sample, state_pool, state_conv, state_ffn_conv, norm1, w_in, pool_w, pool_scale, sgu_w, sgu_b, conv_w, conv_b, cnorm_g, cnorm_b, w_out, norm2, w_up, ffn_conv_w, ffn_conv_b, w_down, final_norm):
    depth = norm1.shape[0]
    d_ff = w_down.shape[1]
    d_b = (w_in.shape[-1] - pool_scale.shape[-1] - 2 * conv_w.shape[-1]) // 2
    hd = d_b // SGU_HEADS
    n_t = x_sample.shape[1]

    row = lambda a: a[:, None, :]
    wts = dict(
        norm1=row(norm1), w_in=w_in.astype(BF16),
        pool_w=_block_diag(pool_w).astype(BF16), pool_scale=row(pool_scale),
        sgu_w=sgu_w.reshape(depth, SGU_HEADS * SGU_CHUNK, SGU_CHUNK),
        sgu_b=jnp.repeat(jnp.swapaxes(sgu_b, 1, 2), hd, axis=-1),
        conv_w=conv_w, conv_b=row(conv_b), cnorm_g=row(cnorm_g), cnorm_b=row(cnorm_b),
        w_out=w_out.astype(BF16), norm2=row(norm2),
        w_up=w_up.astype(BF16),
        ffn_conv_w=ffn_conv_w, ffn_conv_b=row(ffn_conv_b), w_down=w_down.astype(BF16),
        final_norm=final_norm[None, :])
    w_new = jnp.transpose(sgu_w[:, :, :n_t, :n_t], (0, 2, 3, 1)).reshape(depth, n_t * n_t, SGU_HEADS)
    wts['sgu_w_new'] = jnp.repeat(w_new, hd, axis=-1)
    wts['sgu_b_new'] = jnp.repeat(jnp.swapaxes(sgu_b[:, :, :n_t], 1, 2), hd, axis=-1)

    x = x_prompt
    pool_p, conv_p, ffn_p = [], [], []
    for l in range(depth):
        x, pt, ct, ft = _prompt_layer(x, l, wts, apply_final_norm=(l == depth - 1))
        pool_p.append(pt[:, pt.shape[1] - (max(POOL_WINDOWS) - 1):])
        conv_p.append(ct[:, ct.shape[1] - (conv_w.shape[1] - 1):])
        ffn_p.append(ft[:, ft.shape[1] - (ffn_conv_w.shape[1] - 1):])
    y_prompt = x

    swap = lambda a: jnp.swapaxes(a, -3, -2)
    ys, pool_s, conv_s, ffn_s, v_s = _sample_trunk(
        swap(x_sample), swap(state_pool), swap(state_conv), swap(state_ffn_conv), wts)

    return (y_prompt, swap(ys),
            jnp.stack(pool_p), swap(pool_s),
            jnp.stack(conv_p), swap(conv_s),
            jnp.stack(ffn_p), swap(ffn_s),
            swap(v_s))
```
